```python
import jax, jax.numpy as jnp
from jax import lax
import numpy as np

D_MODEL = 4096
BATCH = 32
SEQ = 256
DEPTH = 2
DEC_BATCH = 2
DEC_SEQ = 1024
PAST_LEN = 512

GRID_W = 64
D_FF = 4 * D_MODEL
DQK_A = 64
DV_A = 2 * DQK_A
N_HEADS_A = D_MODEL // (2 * DV_A)
D_B = 128
N_HEADS_B = D_MODEL // (2 * D_B)
WIN_R = 8
WIN_C = 16
DK_C = 256
DV_C = 256
N_HEADS_C = D_MODEL // DV_C
CHUNK = 128
Q_BLOCK = 128
ROPE_BASE = 10000.0
EPS = 1e-6
N_EVEN = (DEPTH + 1) // 2
N_ODD = DEPTH // 2
W_A = N_HEADS_A * 2 * DQK_A
W_AV = N_HEADS_A * DV_A
W_B = N_HEADS_B * D_B
W_IN_AB = 2 * W_A + W_AV + 3 * W_B
W_OUT_AB = W_AV + W_B
W_C = N_HEADS_C * DK_C
W_CV = N_HEADS_C * DV_C
W_IN_C = 2 * W_C + 2 * W_CV

kernel_name = "hybrid_diff_natten_retention_dit_step"

F32 = jnp.float32


def rmsnorm(x, g):
    xf = x.astype(F32)
    y = xf * lax.rsqrt(jnp.mean(xf * xf, axis=-1, keepdims=True) + EPS)
    return (y * g.astype(F32)).astype(x.dtype)


def modulate(h, shift, scale):
    return h * (1.0 + scale[:, None, :]) + shift[:, None, :]


def grid_positions(T):
    t = jnp.arange(T, dtype=jnp.int32)
    return t // GRID_W, t % GRID_W


def rope_1d(x, pos):
    half = x.shape[-1] // 2
    inv = ROPE_BASE ** (-jnp.arange(half, dtype=F32) / half)
    ang = pos.astype(F32)[:, None] * inv[None, :]
    cos = jnp.cos(ang)[:, None, :]
    sin = jnp.sin(ang)[:, None, :]
    xf = x.astype(F32)
    x1, x2 = xf[..., :half], xf[..., half:]
    return jnp.concatenate([x1 * cos - x2 * sin, x2 * cos + x1 * sin], axis=-1).astype(x.dtype)


def rope_2d(x, rows, cols):
    h = x.shape[-1] // 2
    return jnp.concatenate([rope_1d(x[..., :h], rows), rope_1d(x[..., h:], cols)], axis=-1)


def query_blocks(q):
    B, T = q.shape[:2]
    return q.reshape((B, T // Q_BLOCK, Q_BLOCK) + q.shape[2:]).swapaxes(0, 1)


def merge_blocks(o):
    n, B, qb = o.shape[:3]
    return o.swapaxes(0, 1).reshape((B, n * qb) + o.shape[3:])


def softmax_attn(q, k, v):
    scale = q.shape[-1] ** -0.5

    def blk(qb):
        s = jnp.einsum('bqhd,bkhd->bhqk', qb, k).astype(F32) * scale
        p = jax.nn.softmax(s, axis=-1).astype(v.dtype)
        return jnp.einsum('bhqk,bkhd->bqhd', p, v)

    return merge_blocks(lax.map(blk, query_blocks(q)))


def diff_attn(q, k, v, lam):
    scale = q.shape[-1] ** -0.5

    def blk(qb):
        s = jnp.einsum('bqhd,bkhd->bhqk', qb, k).astype(F32) * scale
        p = jax.nn.softmax(s, axis=-1)
        b, h2, nq, nk = p.shape
        p = p.reshape(b, h2 // 2, 2, nq, nk)
        a = (p[:, :, 0] - lam * p[:, :, 1]).astype(v.dtype)
        return jnp.einsum('bhqk,bkhd->bqhd', a, v)

    return merge_blocks(lax.map(blk, query_blocks(q)))


def diff_lambda_value(lp, lam_init):
    lp = lp.astype(F32)
    return jnp.exp(jnp.sum(lp[0] * lp[1])) - jnp.exp(jnp.sum(lp[2] * lp[3])) + lam_init


def diff_head_out(o, subln_g, lam_init):
    return rmsnorm(o, subln_g) * (1.0 - lam_init)


def neighbourhood_attn(q, k, v, k_ctx, v_ctx, rpb):
    B, T, H, d = q.shape
    R = T // GRID_W
    wr = min(WIN_R, R)
    scale = d ** -0.5
    r_idx = jnp.arange(R, dtype=jnp.int32)
    c_idx = jnp.arange(GRID_W, dtype=jnp.int32)
    row_start = jnp.clip(r_idx - wr // 2, 0, R - wr)
    key_rows = row_start[:, None] + jnp.arange(wr, dtype=jnp.int32)[None, :]
    col_start = jnp.clip(c_idx - WIN_C // 2, 0, GRID_W - WIN_C)
    col_ok = (c_idx[None, :] >= col_start[:, None]) & (c_idx[None, :] < col_start[:, None] + WIN_C)
    qg = q.reshape(B, R, GRID_W, H, d)
    kg = k.reshape(B, R, GRID_W, H, d)[:, key_rows]
    vg = v.reshape(B, R, GRID_W, H, d)[:, key_rows]
    dr = key_rows - r_idx[:, None]
    dc = jnp.clip(c_idx[None, :] - c_idx[:, None], -(WIN_C - 1), WIN_C - 1)
    bias = rpb[:, dr[:, None, :, None] + (WIN_R - 1), dc[None, :, None, :] + (WIN_C - 1)]
    s_loc = jnp.einsum('brqhd,brjkhd->bhrqjk', qg, kg).astype(F32) * scale + bias.astype(F32)[None]
    s_loc = jnp.where(col_ok[:, None, :], s_loc, -jnp.inf).reshape(B, H, R, GRID_W, wr * GRID_W)
    s_ctx = jnp.einsum('brqhd,blhd->bhrql', qg, k_ctx).astype(F32) * scale
    p = jax.nn.softmax(jnp.concatenate([s_loc, s_ctx], axis=-1), axis=-1).astype(v.dtype)
    n_loc = wr * GRID_W
    p_loc = p[..., :n_loc].reshape(B, H, R, GRID_W, wr, GRID_W)
    o = (jnp.einsum('bhrqjk,brjkhd->brqhd', p_loc, vg)
         + jnp.einsum('bhrql,blhd->brqhd', p[..., n_loc:], v_ctx))
    return o.reshape(B, T, H, d)


def retention_chunkwise(q, k, v, log_gamma, s0):
    B, T, H, _ = q.shape
    dv = v.shape[-1]
    n = T // CHUNK
    lg = log_gamma.astype(F32)
    idx = jnp.arange(CHUNK, dtype=F32)
    rel = idx[:, None] - idx[None, :]
    decay_mask = jnp.where(rel >= 0, jnp.exp(lg[:, None, None] * jnp.maximum(rel, 0.0)), 0.0)
    q_decay = jnp.exp(lg[None, :] * (idx[:, None] + 1.0))
    k_decay = jnp.exp(lg[None, :] * (CHUNK - 1.0 - idx[:, None]))
    chunk_decay = jnp.exp(lg * CHUNK)

    def to_chunks(x):
        return x.astype(F32).reshape(B, n, CHUNK, H, x.shape[-1]).swapaxes(0, 1)

    def step(s, inp):
        qc, kc, vc = inp
        att = jnp.einsum('bihd,bjhd->bhij', qc, kc) * decay_mask[None]
        o = (jnp.einsum('bhij,bjhe->bihe', att, vc)
             + jnp.einsum('bihd,bhde->bihe', qc, s) * q_decay[None, :, :, None])
        s = (s * chunk_decay[None, :, None, None]
             + jnp.einsum('bjhd,bjhe->bhde', kc * k_decay[None, :, :, None], vc))
        return s, o

    s_final, o = lax.scan(step, s0.astype(F32), (to_chunks(q), to_chunks(k), to_chunks(v)))
    return o.swapaxes(0, 1).reshape(B, T, H, dv).astype(v.dtype), s_final


def bi_retention(q, k, v, lg_f, lg_b, s0_f, s0_b):
    o_f, s_f = retention_chunkwise(q, k, v, lg_f, s0_f)
    o_b, s_b = retention_chunkwise(jnp.flip(q, 1), jnp.flip(k, 1), jnp.flip(v, 1), lg_b, s0_b)
    return o_f + jnp.flip(o_b, 1), s_f, s_b


def split_ab(p):
    B, T, _ = p.shape
    offs = [W_A, 2 * W_A, 2 * W_A + W_AV, 2 * W_A + W_AV + W_B, 2 * W_A + W_AV + 2 * W_B]
    qa, ka, va, qb, kb, vb = jnp.split(p, offs, axis=-1)
    return (qa.reshape(B, T, 2 * N_HEADS_A, DQK_A), ka.reshape(B, T, 2 * N_HEADS_A, DQK_A),
            va.reshape(B, T, N_HEADS_A, DV_A), qb.reshape(B, T, N_HEADS_B, D_B),
            kb.reshape(B, T, N_HEADS_B, D_B), vb.reshape(B, T, N_HEADS_B, D_B))


def merge_ab(oa, ob, w_out):
    B, T = oa.shape[:2]
    return jnp.concatenate([oa.reshape(B, T, W_AV), ob.reshape(B, T, W_B)], axis=-1) @ w_out


def ab_mixer_context(h, w_in, w_out, lam_p, subln_g, lam_init):
    qa, ka, va, qb, kb, vb = split_ab(h @ w_in)
    lam = diff_lambda_value(lam_p, lam_init)
    oa = diff_head_out(diff_attn(qa, ka, va, lam), subln_g, lam_init)
    ob = softmax_attn(qb, kb, vb)
    return merge_ab(oa, ob, w_out), ka, va, kb, vb


def ab_mixer_latent(h, ka_ctx, va_ctx, kb_ctx, vb_ctx, w_in, w_out, lam_p, subln_g, rpb, lam_init, rows, cols):
    qa, ka, va, qb, kb, vb = split_ab(h @ w_in)
    qa = rope_2d(qa, rows, cols)
    ka = rope_2d(ka, rows, cols)
    lam = diff_lambda_value(lam_p, lam_init)
    oa = diff_attn(qa, jnp.concatenate([ka, ka_ctx.astype(ka.dtype)], axis=1),
                   jnp.concatenate([va, va_ctx.astype(va.dtype)], axis=1), lam)
    oa = diff_head_out(oa, subln_g, lam_init)
    ob = neighbourhood_attn(qb, kb, vb, kb_ctx.astype(kb.dtype), vb_ctx.astype(vb.dtype), rpb)
    return merge_ab(oa, ob, w_out)


def c_mixer(h, s0_f, s0_b, w_in, w_out, lg_f, lg_b, norm_g, rows=None, cols=None):
    B, T, _ = h.shape
    q, k, v, g = jnp.split(h @ w_in, [W_C, 2 * W_C, 2 * W_C + W_CV], axis=-1)
    q = q.reshape(B, T, N_HEADS_C, DK_C)
    k = k.reshape(B, T, N_HEADS_C, DK_C) * (DK_C ** -0.5)
    v = v.reshape(B, T, N_HEADS_C, DV_C)
    if rows is not None:
        q = rope_2d(q, rows, cols)
        k = rope_2d(k, rows, cols)
    y, s_f, s_b = bi_retention(q, k, v, lg_f, lg_b, s0_f, s0_b)
    y = rmsnorm(y, norm_g).reshape(B, T, W_CV)
    return (jax.nn.silu(g) * y) @ w_out, s_f, s_b


def squared_relu_mlp(h, w1, w2):
    return jnp.square(jax.nn.relu(h @ w1)) @ w2


def setup_inputs(seed: int = 0) -> dict:
    key = jax.random.key(seed)
    ks = jax.random.split(key, 32)

    def nrm(k, shape, scale=1.0):
        return jax.random.normal(k, shape, F32) * scale

    def log_decay(k):
        expo = 5.0 + jnp.arange(N_HEADS_C, dtype=F32)[None, :] + 0.25 * jax.random.uniform(k, (N_ODD, N_HEADS_C), F32)
        return jnp.log1p(-jnp.exp2(-expo))

    return {
        "x_prompt": nrm(ks[0], (BATCH, SEQ, D_MODEL)),
        "x_sample": nrm(ks[1], (DEC_BATCH, DEC_SEQ, D_MODEL)),
        "c": nrm(ks[2], (DEC_BATCH, D_MODEL)),
        "cache_a_k": nrm(ks[3], (DEC_BATCH, N_EVEN, PAST_LEN, 2 * N_HEADS_A, DQK_A)),
        "cache_a_v": nrm(ks[4], (DEC_BATCH, N_EVEN, PAST_LEN, N_HEADS_A, DV_A)),
        "cache_b_k": nrm(ks[5], (DEC_BATCH, N_EVEN, PAST_LEN, N_HEADS_B, D_B)),
        "cache_b_v": nrm(ks[6], (DEC_BATCH, N_EVEN, PAST_LEN, N_HEADS_B, D_B)),
        "state_c_fwd": nrm(ks[7], (DEC_BATCH, N_ODD, N_HEADS_C, DK_C, DV_C), 0.5),
        "state_c_bwd": nrm(ks[8], (DEC_BATCH, N_ODD, N_HEADS_C, DK_C, DV_C), 0.5),
        "c_ctx": nrm(ks[9], (D_MODEL,)),
        "norm_mix_g": 1.0 + nrm(ks[10], (DEPTH, D_MODEL), 0.05),
        "norm_mlp_g": 1.0 + nrm(ks[11], (DEPTH, D_MODEL), 0.05),
        "w_mod": nrm(ks[12], (DEPTH, D_MODEL, 6 * D_MODEL), 0.5 * D_MODEL ** -0.5),
        "b_mod": nrm(ks[13], (DEPTH, 6 * D_MODEL), 0.02),
        "w_mlp_in": nrm(ks[14], (DEPTH, D_MODEL, D_FF), D_MODEL ** -0.5),
        "w_mlp_out": nrm(ks[15], (DEPTH, D_FF, D_MODEL), D_FF ** -0.5),
        "w_in_ab": nrm(ks[16], (N_EVEN, D_MODEL, W_IN_AB), D_MODEL ** -0.5),
        "w_out_ab": nrm(ks[17], (N_EVEN, W_OUT_AB, D_MODEL), W_OUT_AB ** -0.5),
        "diff_lambda": nrm(ks[18], (N_EVEN, 4, DQK_A), 0.1),
        "diff_subln_g": 1.0 + nrm(ks[19], (N_EVEN, DV_A), 0.05),
        "nat_rpb": nrm(ks[20], (N_EVEN, N_HEADS_B, 2 * WIN_R - 1, 2 * WIN_C - 1), 0.5),
        "w_in_c": nrm(ks[21], (N_ODD, D_MODEL, W_IN_C), D_MODEL ** -0.5),
        "w_out_c": nrm(ks[22], (N_ODD, W_CV, D_MODEL), W_CV ** -0.5),
        "ret_log_decay_fwd": log_decay(ks[23]),
        "ret_log_decay_bwd": log_decay(ks[24]),
        "ret_norm_g": 1.0 + nrm(ks[25], (N_ODD, DV_C), 0.05),
        "final_norm_g": 1.0 + nrm(ks[26], (D_MODEL,), 0.05),
    }


def reference(x_prompt, x_sample, c, cache_a_k, cache_a_v, cache_b_k, cache_b_v, state_c_fwd, state_c_bwd,
              c_ctx, norm_mix_g, norm_mlp_g, w_mod, b_mod, w_mlp_in, w_mlp_out, w_in_ab, w_out_ab,
              diff_lambda, diff_subln_g, nat_rpb, w_in_c, w_out_c, ret_log_decay_fwd, ret_log_decay_bwd,
              ret_norm_g, final_norm_g):
    xp = x_prompt
    xs = x_sample
    rows, cols = grid_positions(xs.shape[1])
    bp = xp.shape[0]
    new_a_k, new_a_v, new_b_k, new_b_v, new_c_f, new_c_b = [], [], [], [], [], []

    for l in range(DEPTH):
        mod_p = jax.nn.silu(c_ctx)[None, :] @ w_mod[l] + b_mod[l]
        mod_s = jax.nn.silu(c) @ w_mod[l] + b_mod[l]
        sh1_p, sc1_p, g1_p, sh2_p, sc2_p, g2_p = jnp.split(mod_p, 6, axis=-1)
        sh1_s, sc1_s, g1_s, sh2_s, sc2_s, g2_s = jnp.split(mod_s, 6, axis=-1)
        hp = modulate(rmsnorm(xp, norm_mix_g[l]), sh1_p, sc1_p)
        hs = modulate(rmsnorm(xs, norm_mix_g[l]), sh1_s, sc1_s)

        if l % 2 == 0:
            e = l // 2
            lam_init = 0.8 - 0.6 * float(np.exp(-0.3 * l))
            out_p, ka, va, kb, vb = ab_mixer_context(hp, w_in_ab[e], w_out_ab[e], diff_lambda[e],
                                                     diff_subln_g[e], lam_init)
            new_a_k.append(ka)
            new_a_v.append(va)
            new_b_k.append(kb)
            new_b_v.append(vb)
            out_s = ab_mixer_latent(hs, cache_a_k[:, e], cache_a_v[:, e], cache_b_k[:, e], cache_b_v[:, e],
                                    w_in_ab[e], w_out_ab[e], diff_lambda[e], diff_subln_g[e], nat_rpb[e],
                                    lam_init, rows, cols)
        else:
            o = l // 2
            zeros = jnp.zeros((bp, N_HEADS_C, DK_C, DV_C), F32)
            out_p, s_f, s_b = c_mixer(hp, zeros, zeros, w_in_c[o], w_out_c[o], ret_log_decay_fwd[o],
                                      ret_log_decay_bwd[o], ret_norm_g[o])
            new_c_f.append(s_f)
            new_c_b.append(s_b)
            out_s, _, _ = c_mixer(hs, state_c_fwd[:, o], state_c_bwd[:, o], w_in_c[o], w_out_c[o],
                                  ret_log_decay_fwd[o], ret_log_decay_bwd[o], ret_norm_g[o], rows, cols)

        xp = xp + g1_p[:, None, :] * out_p
        xs = xs + g1_s[:, None, :] * out_s
        hp = modulate(rmsnorm(xp, norm_mlp_g[l]), sh2_p, sc2_p)
        hs = modulate(rmsnorm(xs, norm_mlp_g[l]), sh2_s, sc2_s)
        xp = xp + g2_p[:, None, :] * squared_relu_mlp(hp, w_mlp_in[l], w_mlp_out[l])
        xs = xs + g2_s[:, None, :] * squared_relu_mlp(hs, w_mlp_in[l], w_mlp_out[l])

    y_prompt = rmsnorm(xp, final_norm_g)
    y_sample = rmsnorm(xs, final_norm_g)
    new_a_k = jnp.stack(new_a_k, axis=1)
    new_a_v = jnp.stack(new_a_v, axis=1)
    new_b_k = jnp.stack(new_b_k, axis=1)
    new_b_v = jnp.stack(new_b_v, axis=1)
    new_c_f = jnp.stack(new_c_f, axis=1)
    new_c_b = jnp.stack(new_c_b, axis=1)
    return (y_prompt, y_sample, new_a_k, new_a_v, new_b_k, new_b_v, new_c_f, new_c_b)
```

```python
import functools

import jax
import jax.numpy as jnp
import numpy as np
from jax import lax
from jax.experimental import pallas as pl
from jax.experimental.pallas import tpu as pltpu

F32 = jnp.float32
BF16 = jnp.bfloat16

EPS = 1e-6
ROPE_BASE = 10000.0
GRID_W = 64
WIN_R = 8
WIN_C = 16
DQK_A = 64
D_HEAD = 128
DK_C = 256
LANES = 128
MOD_ROWS = 8
VMEM_LIMIT = 56 * 1024 * 1024

NT_DIMS = (((1,), (1,)), ((), ()))


def _params(*sem):
    return pltpu.CompilerParams(dimension_semantics=sem, vmem_limit_bytes=VMEM_LIMIT)


def _silu(x):
    return x / (1.0 + jnp.exp(-x))


def _modvec_kernel(c_ref, w_ref, b_ref, o_ref):
    s = _silu(c_ref[...]).astype(BF16)
    w = w_ref[0].astype(BF16)
    o_ref[0] = jnp.dot(s, w, preferred_element_type=F32) + b_ref[0]


def mod_vectors(cvec, w_mod, b_mod, tn=512):
    depth, d, n = w_mod.shape
    return pl.pallas_call(
        _modvec_kernel,
        grid=(depth, n // tn),
        in_specs=[
            pl.BlockSpec((MOD_ROWS, d), lambda l, j: (0, 0)),
            pl.BlockSpec((1, d, tn), lambda l, j: (l, 0, j)),
            pl.BlockSpec((1, 1, tn), lambda l, j: (l, 0, j)),
        ],
        out_specs=pl.BlockSpec((1, MOD_ROWS, tn), lambda l, j: (l, 0, j)),
        out_shape=jax.ShapeDtypeStruct((depth, MOD_ROWS, n), F32),
        compiler_params=_params("parallel", "parallel"),
        name="mod_vectors",
    )(cvec, w_mod, b_mod.reshape(depth, 1, n))


def _norm_mod_kernel(x_ref, g_ref, sh_ref, sc_ref, o_ref):
    x = x_ref[...]
    y = x * lax.rsqrt(jnp.mean(x * x, axis=-1, keepdims=True) + EPS) * g_ref[...]
    o_ref[...] = (y * (1.0 + sc_ref[0]) + sh_ref[0]).astype(o_ref.dtype)


def norm_modulate(x, g, shift, scale, rows_per_group, tr=256):
    m, d = x.shape
    grp = lambda i: ((i * tr) // rows_per_group, 0, 0)
    return pl.pallas_call(
        _norm_mod_kernel,
        grid=(m // tr,),
        in_specs=[
            pl.BlockSpec((tr, d), lambda i: (i, 0)),
            pl.BlockSpec((1, d), lambda i: (0, 0)),
            pl.BlockSpec((1, 1, d), grp),
            pl.BlockSpec((1, 1, d), grp),
        ],
        out_specs=pl.BlockSpec((tr, d), lambda i: (i, 0)),
        out_shape=jax.ShapeDtypeStruct((m, d), BF16),
        compiler_params=_params("parallel"),
        name="norm_modulate",
    )(x, g.reshape(1, d), shift, scale)


def _norm_kernel(x_ref, g_ref, o_ref):
    x = x_ref[...]
    o_ref[...] = x * lax.rsqrt(jnp.mean(x * x, axis=-1, keepdims=True) + EPS) * g_ref[...]


def rms_norm(x, g, tr=256):
    m, d = x.shape
    return pl.pallas_call(
        _norm_kernel,
        grid=(m // tr,),
        in_specs=[pl.BlockSpec((tr, d), lambda i: (i, 0)), pl.BlockSpec((1, d), lambda i: (0, 0))],
        out_specs=pl.BlockSpec((tr, d), lambda i: (i, 0)),
        out_shape=jax.ShapeDtypeStruct((m, d), F32),
        compiler_params=_params("parallel"),
        name="rms_norm",
    )(x, g.reshape(1, d))


def _mm_kernel(*refs, mode, nk):
    if mode == "resgate":
        x_ref, w_ref, res_ref, gate_ref, o_ref = refs[:5]
        scratch = refs[5:]
    else:
        x_ref, w_ref, o_ref = refs[:3]
        scratch = refs[3:]

    def epilogue(acc):
        if mode == "relu2":
            r = jnp.maximum(acc, 0.0)
            out = r * r
        elif mode == "resgate":
            out = res_ref[...] + gate_ref[0] * acc
        else:
            out = acc
        o_ref[...] = out.astype(o_ref.dtype)

    if nk == 1:
        epilogue(jnp.dot(x_ref[...], w_ref[...], preferred_element_type=F32))
        return

    acc_ref = scratch[0]
    k = pl.program_id(2)

    @pl.when(k == 0)
    def _():
        acc_ref[...] = jnp.zeros_like(acc_ref)

    acc_ref[...] += jnp.dot(x_ref[...], w_ref[...], preferred_element_type=F32)

    @pl.when(k == nk - 1)
    def _():
        epilogue(acc_ref[...])


def matmul(x, w, *, mode="plain", out_dtype=F32, res=None, gate=None, rows_per_group=None,
           tm=1024, tn=512, tk=4096):
    m, kdim = x.shape
    n = w.shape[1]
    tk = min(tk, kdim)
    nk = kdim // tk
    in_specs = [
        pl.BlockSpec((tm, tk), lambda i, j, k: (i, k)),
        pl.BlockSpec((tk, tn), lambda i, j, k: (k, j)),
    ]
    args = [x, w]
    if mode == "resgate":
        in_specs += [
            pl.BlockSpec((tm, tn), lambda i, j, k: (i, j)),
            pl.BlockSpec((1, 1, tn), lambda i, j, k: ((i * tm) // rows_per_group, 0, j)),
        ]
        args += [res, gate]
    return pl.pallas_call(
        functools.partial(_mm_kernel, mode=mode, nk=nk),
        grid=(m // tm, n // tn, nk),
        in_specs=in_specs,
        out_specs=pl.BlockSpec((tm, tn), lambda i, j, k: (i, j)),
        out_shape=jax.ShapeDtypeStruct((m, n), out_dtype),
        scratch_shapes=[pltpu.VMEM((tm, tn), F32)] if nk > 1 else [],
        compiler_params=_params("parallel", "parallel", "arbitrary"),
        name="matmul_" + mode,
    )(*args)


def _softmax_parts(scores):
    m = functools.reduce(jnp.maximum, [jnp.max(s, axis=-1, keepdims=True) for s in scores])
    es = [jnp.exp(s - m) for s in scores]
    den = functools.reduce(jnp.add, [jnp.sum(e, axis=-1, keepdims=True) for e in es])
    inv = 1.0 / den
    return [e * inv for e in es]


def _scores(q_bf, k_bf, scale):
    return lax.dot_general(q_bf, k_bf, NT_DIMS, preferred_element_type=F32) * scale


def _diff_lambda(lp_ref, lam_init):
    lp = lp_ref[...]
    a = jnp.sum(lp[0:1] * lp[1:2], axis=-1, keepdims=True)
    b = jnp.sum(lp[2:3] * lp[3:4], axis=-1, keepdims=True)
    return jnp.exp(a) - jnp.exp(b) + lam_init


def _diff_attention(q, ks, vs, lam, subln_g, lam_init):
    lane = lax.broadcasted_iota(jnp.int32, q.shape, 1)
    q1 = jnp.where(lane < DQK_A, q, 0.0).astype(BF16)
    q2 = jnp.where(lane >= DQK_A, q, 0.0).astype(BF16)
    kb = [k.astype(BF16) for k in ks]
    scale = DQK_A ** -0.5
    p1 = _softmax_parts([_scores(q1, k, scale) for k in kb])
    p2 = _softmax_parts([_scores(q2, k, scale) for k in kb])
    o = None
    for a1, a2, v in zip(p1, p2, vs):
        a = (a1 - lam * a2).astype(BF16)
        t = jnp.dot(a, v.astype(BF16), preferred_element_type=F32)
        o = t if o is None else o + t
    y = o * lax.rsqrt(jnp.mean(o * o, axis=-1, keepdims=True) + EPS) * subln_g
    return y * (1.0 - lam_init)


def _softmax_attention(q, ks, vs, biases):
    qb = q.astype(BF16)
    scale = D_HEAD ** -0.5
    scores = []
    for k, bias in zip(ks, biases):
        s = _scores(qb, k.astype(BF16), scale)
        scores.append(s if bias is None else s + bias)
    ps = _softmax_parts(scores)
    o = None
    for p, v in zip(ps, vs):
        t = jnp.dot(p.astype(BF16), v.astype(BF16), preferred_element_type=F32)
        o = t if o is None else o + t
    return o


def _rope_a(x, cos, sin_lo, sin_hi):
    half = DQK_A // 4
    return x * cos + pltpu.roll(x, LANES - half, 1) * sin_lo + pltpu.roll(x, half, 1) * sin_hi


def _ctx_ab_kernel(qa_ref, ka_ref, va_ref, qb_ref, kb_ref, vb_ref, lp_ref, g_ref, o_ref, *, lam_init):
    lam = _diff_lambda(lp_ref, lam_init)
    oa = _diff_attention(qa_ref[...], [ka_ref[...]], [va_ref[...]], lam, g_ref[...], lam_init)
    ob = _softmax_attention(qb_ref[...], [kb_ref[...]], [vb_ref[...]], [None])
    o_ref[:, :D_HEAD] = oa.astype(o_ref.dtype)
    o_ref[:, D_HEAD:] = ob.astype(o_ref.dtype)


def ctx_ab_attention(p, seq, n_heads, diff_lambda, subln_g, lam_init):
    m = p.shape[0]
    nb = m // seq
    blk = lambda part: pl.BlockSpec((seq, D_HEAD), lambda b, h, part=part: (b, part * n_heads + h))
    return pl.pallas_call(
        functools.partial(_ctx_ab_kernel, lam_init=lam_init),
        grid=(nb, n_heads),
        in_specs=[blk(0), blk(1), blk(2), blk(3), blk(4), blk(5),
                  pl.BlockSpec((4, DQK_A), lambda b, h: (0, 0)),
                  pl.BlockSpec((1, D_HEAD), lambda b, h: (0, 0))],
        out_specs=pl.BlockSpec((seq, 2 * D_HEAD), lambda b, h: (b, h)),
        out_shape=jax.ShapeDtypeStruct((m, n_heads * 2 * D_HEAD), BF16),
        compiler_params=_params("parallel", "parallel"),
        name="ctx_ab_attention",
    )(p, p, p, p, p, p, diff_lambda, subln_g.reshape(1, D_HEAD))


def _na_bias_kernel(rpb_ref, o_ref, *, n_rows):
    h = pl.program_id(0)
    nd = 2 * WIN_C - 1
    ne = 2 * WIN_R - 1
    shape = (GRID_W, LANES)
    lane = lax.broadcasted_iota(jnp.int32, shape, 1)
    qc = lax.broadcasted_iota(jnp.int32, shape, 0)
    kc = lane & (GRID_W - 1)
    diff = kc - qc + (WIN_C - 1)
    cs = jnp.clip(qc - WIN_C // 2, 0, GRID_W - WIN_C)
    col_ok = (kc >= cs) & (kc < cs + WIN_C)
    low_half = lane < GRID_W
    neg = jnp.full(shape, -jnp.inf, F32)

    toeplitz = []
    for e in range(ne):
        acc = jnp.zeros(shape, F32)
        for d in range(nd):
            acc = jnp.where(diff == d, rpb_ref[h, e * nd + d], acc)
        toeplitz.append(jnp.where(col_ok, acc, neg))

    wr = min(WIN_R, n_rows)
    for r in range(n_rows):
        rs = min(max(r - wr // 2, 0), n_rows - wr)
        for pair in range(n_rows // 2):
            kr0, kr1 = 2 * pair, 2 * pair + 1
            a = toeplitz[kr0 - r + WIN_R - 1] if rs <= kr0 < rs + wr else neg
            b = toeplitz[kr1 - r + WIN_R - 1] if rs <= kr1 < rs + wr else neg
            blk = a if a is b else jnp.where(low_half, a, b)
            o_ref[0, r * GRID_W:(r + 1) * GRID_W, pair * LANES:(pair + 1) * LANES] = blk


def na_bias_table(rpb, n_rows):
    n_heads = rpb.shape[0]
    t = n_rows * GRID_W
    return pl.pallas_call(
        functools.partial(_na_bias_kernel, n_rows=n_rows),
        grid=(n_heads,),
        in_specs=[pl.BlockSpec(memory_space=pltpu.SMEM)],
        out_specs=pl.BlockSpec((1, t, t), lambda h: (h, 0, 0)),
        out_shape=jax.ShapeDtypeStruct((n_heads, t, t), F32),
        compiler_params=_params("parallel"),
        name="na_bias_table",
    )(rpb.reshape(n_heads, -1))


def _lat_ab_kernel(qa_ref, ka_ref, va_ref, qb_ref, kb_ref, vb_ref,
                   cak_ref, cav_ref, cbk_ref, cbv_ref, bias_ref,
                   qcos_ref, qslo_ref, qshi_ref, kcos_ref, kslo_ref, kshi_ref,
                   lp_ref, g_ref, o_ref, *, lam_init):
    lam = _diff_lambda(lp_ref, lam_init)
    q = _rope_a(qa_ref[...], qcos_ref[...], qslo_ref[...], qshi_ref[...])
    k = _rope_a(ka_ref[...], kcos_ref[...], kslo_ref[...], kshi_ref[...])
    oa = _diff_attention(q, [k, cak_ref[0]], [va_ref[...], cav_ref[0]], lam, g_ref[...], lam_init)
    ob = _softmax_attention(qb_ref[...], [kb_ref[...], cbk_ref[0]], [vb_ref[...], cbv_ref[0]],
                            [bias_ref[0], None])
    o_ref[:, :D_HEAD] = oa.astype(o_ref.dtype)
    o_ref[:, D_HEAD:] = ob.astype(o_ref.dtype)


def lat_ab_attention(p, seq, n_heads, cache_a_k, cache_a_v, cache_b_k, cache_b_v, bias, rope,
                     diff_lambda, subln_g, lam_init, tq=256):
    m = p.shape[0]
    nb = m // seq
    nq = seq // tq
    past = cache_a_k.shape[1]
    qblk = lambda part: pl.BlockSpec((tq, D_HEAD), lambda b, h, i, part=part: (b * nq + i, part * n_heads + h))
    kblk = lambda part: pl.BlockSpec((seq, D_HEAD), lambda b, h, i, part=part: (b, part * n_heads + h))
    cblk = pl.BlockSpec((1, past, D_HEAD), lambda b, h, i: (b, 0, h))
    qtab = pl.BlockSpec((tq, D_HEAD), lambda b, h, i: (i, 0))
    ktab = pl.BlockSpec((seq, D_HEAD), lambda b, h, i: (0, 0))
    cos, sin_lo, sin_hi = rope
    return pl.pallas_call(
        functools.partial(_lat_ab_kernel, lam_init=lam_init),
        grid=(nb, n_heads, nq),
        in_specs=[qblk(0), kblk(1), kblk(2), qblk(3), kblk(4), kblk(5),
                  cblk, cblk, cblk, cblk,
                  pl.BlockSpec((1, tq, seq), lambda b, h, i: (h, i, 0)),
                  qtab, qtab, qtab, ktab, ktab, ktab,
                  pl.BlockSpec((4, DQK_A), lambda b, h, i: (0, 0)),
                  pl.BlockSpec((1, D_HEAD), lambda b, h, i: (0, 0))],
        out_specs=pl.BlockSpec((tq, 2 * D_HEAD), lambda b, h, i: (b * nq + i, h)),
        out_shape=jax.ShapeDtypeStruct((m, n_heads * 2 * D_HEAD), BF16),
        compiler_params=_params("parallel", "parallel", "parallel"),
        name="lat_ab_attention",
    )(p, p, p, p, p, p, cache_a_k, cache_a_v, cache_b_k, cache_b_v, bias,
      cos, sin_lo, sin_hi, cos, sin_lo, sin_hi, diff_lambda, subln_g.reshape(1, D_HEAD))


def _decay_mask(row0, tq, tk, lg_f, lg_b):
    i = lax.broadcasted_iota(jnp.int32, (tq, tk), 0) + row0
    j = lax.broadcasted_iota(jnp.int32, (tq, tk), 1)
    rel = (i - j).astype(F32)
    fwd = jnp.where(rel >= 0.0, jnp.exp(lg_f * jnp.maximum(rel, 0.0)), 0.0)
    bwd = jnp.where(rel <= 0.0, jnp.exp(lg_b * jnp.maximum(-rel, 0.0)), 0.0)
    return fwd + bwd


def _gated_head_norm(o, g, norm_g):
    y = o * lax.rsqrt(jnp.mean(o * o, axis=-1, keepdims=True) + EPS) * norm_g
    return _silu(g) * y


def _ctx_ret_kernel(lg_ref, q_ref, k_ref, v_ref, g_ref, ng_ref, o_ref, sf_ref, sb_ref):
    h = pl.program_id(1)
    lg_f = lg_ref[0, h]
    lg_b = lg_ref[1, h]
    t = q_ref.shape[0]
    q = q_ref[...].astype(BF16)
    k = k_ref[...] * (DK_C ** -0.5)
    v = v_ref[...].astype(BF16)
    s = lax.dot_general(q, k.astype(BF16), NT_DIMS, preferred_element_type=F32)
    att = (s * _decay_mask(0, t, t, lg_f, lg_b)).astype(BF16)
    o = jnp.dot(att, v, preferred_element_type=F32)
    o_ref[...] = _gated_head_norm(o, g_ref[...], ng_ref[...]).astype(o_ref.dtype)
    pos = lax.broadcasted_iota(jnp.int32, (t, 1), 0).astype(F32)
    kf = (k * jnp.exp(lg_f * (t - 1.0 - pos))).T.astype(BF16)
    kb = (k * jnp.exp(lg_b * pos)).T.astype(BF16)
    sf_ref[...] = jnp.dot(kf, v, preferred_element_type=F32)
    sb_ref[...] = jnp.dot(kb, v, preferred_element_type=F32)


def ctx_retention(p, seq, n_heads, lg, norm_g):
    m = p.shape[0]
    nb = m // seq
    blk = lambda part: pl.BlockSpec((seq, DK_C), lambda b, h, part=part: (b, part * n_heads + h))
    state = pl.BlockSpec((None, None, None, DK_C, DK_C), lambda b, h: (b, 0, h, 0, 0))
    return pl.pallas_call(
        _ctx_ret_kernel,
        grid=(nb, n_heads),
        in_specs=[pl.BlockSpec(memory_space=pltpu.SMEM), blk(0), blk(1), blk(2), blk(3),
                  pl.BlockSpec((1, DK_C), lambda b, h: (0, 0))],
        out_specs=[pl.BlockSpec((seq, DK_C), lambda b, h: (b, h)), state, state],
        out_shape=[jax.ShapeDtypeStruct((m, n_heads * DK_C), BF16),
                   jax.ShapeDtypeStruct((nb, 1, n_heads, DK_C, DK_C), F32),
                   jax.ShapeDtypeStruct((nb, 1, n_heads, DK_C, DK_C), F32)],
        compiler_params=_params("parallel", "parallel"),
        name="ctx_retention",
    )(lg, p, p, p, p, norm_g.reshape(1, DK_C))


def _rope_c(x, cos, sin):
    halves = []
    for c in range(DK_C // LANES):
        sl = slice(c * LANES, (c + 1) * LANES)
        xs = x[:, sl]
        halves.append(xs * cos[:, sl] + pltpu.roll(xs, LANES // 2, 1) * sin[:, sl])
    return jnp.concatenate(halves, axis=-1)


def _lat_ret_kernel(lg_ref, q_ref, k_ref, v_ref, g_ref, s0f_ref, s0b_ref, ng_ref,
                    qcos_ref, qsin_ref, kcos_ref, ksin_ref, o_ref, *, tq):
    h = pl.program_id(1)
    row0 = pl.program_id(2) * tq
    lg_f = lg_ref[0, h]
    lg_b = lg_ref[1, h]
    t = k_ref.shape[0]
    q = _rope_c(q_ref[...], qcos_ref[...], qsin_ref[...]).astype(BF16)
    k = _rope_c(k_ref[...] * (DK_C ** -0.5), kcos_ref[...], ksin_ref[...]).astype(BF16)
    s = lax.dot_general(q, k, NT_DIMS, preferred_element_type=F32)
    att = (s * _decay_mask(row0, tq, t, lg_f, lg_b)).astype(BF16)
    o = jnp.dot(att, v_ref[...].astype(BF16), preferred_element_type=F32)
    pos = (lax.broadcasted_iota(jnp.int32, (tq, 1), 0) + row0).astype(F32)
    o += jnp.dot(q, s0f_ref[...].astype(BF16), preferred_element_type=F32) * jnp.exp(lg_f * (pos + 1.0))
    o += jnp.dot(q, s0b_ref[...].astype(BF16), preferred_element_type=F32) * jnp.exp(lg_b * (t - pos))
    o_ref[...] = _gated_head_norm(o, g_ref[...], ng_ref[...]).astype(o_ref.dtype)


def lat_retention(p, seq, n_heads, lg, norm_g, s0_f, s0_b, rope, tq=256):
    m = p.shape[0]
    nb = m // seq
    nq = seq // tq
    qblk = lambda part: pl.BlockSpec((tq, DK_C), lambda b, h, i, part=part: (b * nq + i, part * n_heads + h))
    kblk = lambda part: pl.BlockSpec((seq, DK_C), lambda b, h, i, part=part: (b, part * n_heads + h))
    state = pl.BlockSpec((None, None, None, DK_C, DK_C), lambda b, h, i: (b, 0, h, 0, 0))
    qtab = pl.BlockSpec((tq, DK_C), lambda b, h, i: (i, 0))
    ktab = pl.BlockSpec((seq, DK_C), lambda b, h, i: (0, 0))
    cos, sin = rope
    return pl.pallas_call(
        functools.partial(_lat_ret_kernel, tq=tq),
        grid=(nb, n_heads, nq),
        in_specs=[pl.BlockSpec(memory_space=pltpu.SMEM), qblk(0), kblk(1), kblk(2), qblk(3),
                  state, state, pl.BlockSpec((1, DK_C), lambda b, h, i: (0, 0)),
                  qtab, qtab, ktab, ktab],
        out_specs=pl.BlockSpec((tq, DK_C), lambda b, h, i: (b * nq + i, h)),
        out_shape=jax.ShapeDtypeStruct((m, n_heads * DK_C), BF16),
        compiler_params=_params("parallel", "parallel", "parallel"),
        name="lat_retention",
    )(lg, p, p, p, p, s0_f, s0_b, norm_g.reshape(1, DK_C), cos, sin, cos, sin)


def _rope_tables(seq, width, group):
    half = group // 2
    t = np.arange(seq)
    rows, cols = t // GRID_W, t % GRID_W
    lane = np.arange(width)
    inv = (ROPE_BASE ** (-jnp.arange(half, dtype=F32) / half))[lane % half]
    use_cols = (lane // group) % 2 == 1
    pos = np.where(use_cols[None, :], cols[:, None], rows[:, None])
    ang = jnp.asarray(pos, dtype=jnp.int32).astype(F32) * inv[None, :]
    first = jnp.asarray((lane % group) < half)
    return jnp.cos(ang), jnp.sin(ang), first


def rope_tables_a(seq):
    cos, sin, first = _rope_tables(seq, D_HEAD, DQK_A // 2)
    return cos, jnp.where(first, -sin, 0.0), jnp.where(first, 0.0, sin)


def rope_tables_c(seq):
    cos, sin, first = _rope_tables(seq, DK_C, DK_C // 2)
    return cos, jnp.where(first, -sin, sin)


def kernel(x_prompt, x_sample, c, cache_a_k, cache_a_v, cache_b_k, cache_b_v, state_c_fwd, state_c_bwd,
           c_ctx, norm_mix_g, norm_mlp_g, w_mod, b_mod, w_mlp_in, w_mlp_out, w_in_ab, w_out_ab,
           diff_lambda, diff_subln_g, nat_rpb, w_in_c, w_out_c, ret_log_decay_fwd, ret_log_decay_bwd,
           ret_norm_g, final_norm_g):
    bp, seq_p, d = x_prompt.shape
    bs, seq_s, _ = x_sample.shape
    depth = w_mod.shape[0]
    n_heads_ab = w_out_ab.shape[1] // (2 * D_HEAD)
    n_heads_c = w_out_c.shape[1] // DK_C
    past = cache_a_k.shape[2]
    w_ab = n_heads_ab * D_HEAD

    xp = x_prompt.reshape(bp * seq_p, d)
    xs = x_sample.reshape(bs * seq_s, d)
    rows_p = bp * seq_p

    cvec = jnp.concatenate([c_ctx[None, :], c, jnp.zeros((MOD_ROWS - 1 - bs, d), F32)], axis=0)
    mod = mod_vectors(cvec, w_mod, b_mod).reshape(depth, MOD_ROWS, 6, 1, d)

    new_a_k, new_a_v, new_b_k, new_b_v, new_c_f, new_c_b = [], [], [], [], [], []

    for l in range(depth):
        sh1_p, sc1_p, g1_p, sh2_p, sc2_p, g2_p = (mod[l, 0:1, i] for i in range(6))
        sh1_s, sc1_s, g1_s, sh2_s, sc2_s, g2_s = (mod[l, 1:1 + bs, i] for i in range(6))
        hp = norm_modulate(xp, norm_mix_g[l], sh1_p, sc1_p, rows_p)
        hs = norm_modulate(xs, norm_mix_g[l], sh1_s, sc1_s, seq_s)

        if l % 2 == 0:
            e = l // 2
            lam_init = 0.8 - 0.6 * float(np.exp(-0.3 * l))
            w_in = w_in_ab[e].astype(BF16)
            w_out = (w_out_ab[e].reshape(2, n_heads_ab, D_HEAD, d).transpose(1, 0, 2, 3)
                     .reshape(2 * w_ab, d).astype(BF16))
            pp = matmul(hp, w_in)
            ps = matmul(hs, w_in)
            new_a_k.append(pp[:, w_ab:2 * w_ab].reshape(bp, seq_p, 2 * n_heads_ab, DQK_A))
            new_a_v.append(pp[:, 2 * w_ab:3 * w_ab].reshape(bp, seq_p, n_heads_ab, D_HEAD))
            new_b_k.append(pp[:, 4 * w_ab:5 * w_ab].reshape(bp, seq_p, n_heads_ab, D_HEAD))
            new_b_v.append(pp[:, 5 * w_ab:6 * w_ab].reshape(bp, seq_p, n_heads_ab, D_HEAD))
            mix_p = ctx_ab_attention(pp, seq_p, n_heads_ab, diff_lambda[e], diff_subln_g[e], lam_init)
            bias = na_bias_table(nat_rpb[e], seq_s // GRID_W)
            mix_s = lat_ab_attention(
                ps, seq_s, n_heads_ab,
                cache_a_k[:, e].reshape(bs, past, w_ab), cache_a_v[:, e].reshape(bs, past, w_ab),
                cache_b_k[:, e].reshape(bs, past, w_ab), cache_b_v[:, e].reshape(bs, past, w_ab),
                bias, rope_tables_a(seq_s), diff_lambda[e], diff_subln_g[e], lam_init)
        else:
            o = l // 2
            w_in = w_in_c[o].astype(BF16)
            w_out = w_out_c[o].astype(BF16)
            lg = jnp.stack([ret_log_decay_fwd[o], ret_log_decay_bwd[o]])
            pp = matmul(hp, w_in)
            ps = matmul(hs, w_in)
            mix_p, s_f, s_b = ctx_retention(pp, seq_p, n_heads_c, lg, ret_norm_g[o])
            new_c_f.append(s_f)
            new_c_b.append(s_b)
            mix_s = lat_retention(ps, seq_s, n_heads_c, lg, ret_norm_g[o],
                                  state_c_fwd[:, o:o + 1], state_c_bwd[:, o:o + 1], rope_tables_c(seq_s))

        xp = matmul(mix_p, w_out, mode="resgate", res=xp, gate=g1_p, rows_per_group=rows_p)
        xs = matmul(mix_s, w_out, mode="resgate", res=xs, gate=g1_s, rows_per_group=seq_s)

        hp = norm_modulate(xp, norm_mlp_g[l], sh2_p, sc2_p, rows_p)
        hs = norm_modulate(xs, norm_mlp_g[l], sh2_s, sc2_s, seq_s)
        w1 = w_mlp_in[l].astype(BF16)
        w2 = w_mlp_out[l].astype(BF16)
        up = matmul(hp, w1, mode="relu2", out_dtype=BF16)
        us = matmul(hs, w1, mode="relu2", out_dtype=BF16)
        xp = matmul(up, w2, mode="resgate", res=xp, gate=g2_p, rows_per_group=rows_p)
        xs = matmul(us, w2, mode="resgate", res=xs, gate=g2_s, rows_per_group=seq_s)

    y_prompt = rms_norm(xp, final_norm_g).reshape(bp, seq_p, d)
    y_sample = rms_norm(xs, final_norm_g).reshape(bs, seq_s, d)
    return (y_prompt, y_sample,
            jnp.stack(new_a_k, axis=1), jnp.stack(new_a_v, axis=1),
            jnp.stack(new_b_k, axis=1), jnp.stack(new_b_v, axis=1),
            jnp.concatenate(new_c_f, axis=1), jnp.concatenate(new_c_b, axis=1))
```

```python
import functools

import jax
import jax.numpy as jnp
import numpy as np
from jax import lax
from jax.experimental import pallas as pl
from jax.experimental.pallas import tpu as pltpu

F32 = jnp.float32
BF16 = jnp.bfloat16

EPS = 1e-6
ROPE_BASE = 10000.0
GRID_W = 64
WIN_R = 8
WIN_C = 16
DQK_A = 64
D_HEAD = 128
DK_C = 256
LANES = 128
MOD_ROWS = 8
VMEM_LIMIT = 56 * 1024 * 1024

NT_DIMS = (((1,), (1,)), ((), ()))


def _params(*sem):
    return pltpu.CompilerParams(dimension_semantics=sem, vmem_limit_bytes=VMEM_LIMIT)


def _silu(x):
    return x / (1.0 + jnp.exp(-x))


def _modvec_kernel(c_ref, w_ref, b_ref, o_ref):
    s = _silu(c_ref[...]).astype(BF16)
    w = w_ref[0].astype(BF16)
    o_ref[0] = jnp.dot(s, w, preferred_element_type=F32) + b_ref[0]


def mod_vectors(cvec, w_mod, b_mod, tn=512):
    depth, d, n = w_mod.shape
    return pl.pallas_call(
        _modvec_kernel,
        grid=(depth, n // tn),
        in_specs=[
            pl.BlockSpec((MOD_ROWS, d), lambda l, j: (0, 0)),
            pl.BlockSpec((1, d, tn), lambda l, j: (l, 0, j)),
            pl.BlockSpec((1, 1, tn), lambda l, j: (l, 0, j)),
        ],
        out_specs=pl.BlockSpec((1, MOD_ROWS, tn), lambda l, j: (l, 0, j)),
        out_shape=jax.ShapeDtypeStruct((depth, MOD_ROWS, n), F32),
        compiler_params=_params("parallel", "parallel"),
        name="mod_vectors",
    )(cvec, w_mod, b_mod.reshape(depth, 1, n))


def _norm_mod_kernel(x_ref, g_ref, sh_ref, sc_ref, o_ref):
    x = x_ref[...]
    y = x * lax.rsqrt(jnp.mean(x * x, axis=-1, keepdims=True) + EPS) * g_ref[...]
    o_ref[...] = (y * (1.0 + sc_ref[0]) + sh_ref[0]).astype(o_ref.dtype)


def norm_modulate(x, g, shift, scale, rows_per_group, tr=256):
    m, d = x.shape
    grp = lambda i: ((i * tr) // rows_per_group, 0, 0)
    return pl.pallas_call(
        _norm_mod_kernel,
        grid=(m // tr,),
        in_specs=[
            pl.BlockSpec((tr, d), lambda i: (i, 0)),
            pl.BlockSpec((1, d), lambda i: (0, 0)),
            pl.BlockSpec((1, 1, d), grp),
            pl.BlockSpec((1, 1, d), grp),
        ],
        out_specs=pl.BlockSpec((tr, d), lambda i: (i, 0)),
        out_shape=jax.ShapeDtypeStruct((m, d), BF16),
        compiler_params=_params("parallel"),
        name="norm_modulate",
    )(x, g.reshape(1, d), shift, scale)


def _norm_kernel(x_ref, g_ref, o_ref):
    x = x_ref[...]
    o_ref[...] = x * lax.rsqrt(jnp.mean(x * x, axis=-1, keepdims=True) + EPS) * g_ref[...]


def rms_norm(x, g, tr=256):
    m, d = x.shape
    return pl.pallas_call(
        _norm_kernel,
        grid=(m // tr,),
        in_specs=[pl.BlockSpec((tr, d), lambda i: (i, 0)), pl.BlockSpec((1, d), lambda i: (0, 0))],
        out_specs=pl.BlockSpec((tr, d), lambda i: (i, 0)),
        out_shape=jax.ShapeDtypeStruct((m, d), F32),
        compiler_params=_params("parallel"),
        name="rms_norm",
    )(x, g.reshape(1, d))


def _mm_kernel(*refs, mode, nk, n_pairs):
    x_refs = refs[:n_pairs]
    w_refs = refs[n_pairs:2 * n_pairs]
    refs = refs[2 * n_pairs:]
    if mode == "resgate":
        res_ref, gate_ref, o_ref = refs[:3]
        scratch = refs[3:]
    else:
        o_ref = refs[0]
        scratch = refs[1:]

    def product():
        acc = None
        for x_ref, w_ref in zip(x_refs, w_refs):
            t = jnp.dot(x_ref[...], w_ref[...].astype(BF16), preferred_element_type=F32)
            acc = t if acc is None else acc + t
        return acc

    def epilogue(acc):
        if mode == "relu2":
            r = jnp.maximum(acc, 0.0)
            out = r * r
        elif mode == "resgate":
            out = res_ref[...] + gate_ref[0] * acc
        else:
            out = acc
        o_ref[...] = out.astype(o_ref.dtype)

    if nk == 1:
        epilogue(product())
        return

    acc_ref = scratch[0]
    k = pl.program_id(2)

    @pl.when(k == 0)
    def _():
        acc_ref[...] = jnp.zeros_like(acc_ref)

    acc_ref[...] += product()

    @pl.when(k == nk - 1)
    def _():
        epilogue(acc_ref[...])


def matmul(xs, w, layer, *, mode="plain", out_dtype=F32, res=None, gate=None, rows_per_group=None,
           tm=1024, tn=512, tk=4096):
    xs = list(xs) if isinstance(xs, (list, tuple)) else [xs]
    n_pairs = len(xs)
    m, kpiece = xs[0].shape
    n = w.shape[2]
    tk = min(tk, kpiece)
    nk = kpiece // tk
    assert n_pairs == 1 or nk == 1
    in_specs = [pl.BlockSpec((tm, tk), lambda i, j, k: (i, k)) for _ in xs]
    in_specs += [pl.BlockSpec((None, tk, tn), lambda i, j, k, piece=piece: (layer, k + piece, j))
                 for piece in range(n_pairs)]
    args = xs + [w] * n_pairs
    if mode == "resgate":
        in_specs += [
            pl.BlockSpec((tm, tn), lambda i, j, k: (i, j)),
            pl.BlockSpec((1, 1, tn), lambda i, j, k: ((i * tm) // rows_per_group, 0, j)),
        ]
        args += [res, gate]
    return pl.pallas_call(
        functools.partial(_mm_kernel, mode=mode, nk=nk, n_pairs=n_pairs),
        grid=(m // tm, n // tn, nk),
        in_specs=in_specs,
        out_specs=pl.BlockSpec((tm, tn), lambda i, j, k: (i, j)),
        out_shape=jax.ShapeDtypeStruct((m, n), out_dtype),
        scratch_shapes=[pltpu.VMEM((tm, tn), F32)] if nk > 1 else [],
        compiler_params=_params("parallel", "parallel", "arbitrary"),
        name="matmul_" + mode,
    )(*args)


def _softmax_parts(scores):
    m = functools.reduce(jnp.maximum, [jnp.max(s, axis=-1, keepdims=True) for s in scores])
    es = [jnp.exp(s - m) for s in scores]
    den = functools.reduce(jnp.add, [jnp.sum(e, axis=-1, keepdims=True) for e in es])
    inv = 1.0 / den
    return [e * inv for e in es]


def _scores(q_bf, k_bf, scale, keys_transposed=False):
    if keys_transposed:
        return jnp.dot(q_bf, k_bf, preferred_element_type=F32) * scale
    return lax.dot_general(q_bf, k_bf, NT_DIMS, preferred_element_type=F32) * scale


def _diff_lambda(lp_ref, lam_init):
    lp = lp_ref[...]
    a = jnp.sum(lp[0:1] * lp[1:2], axis=-1, keepdims=True)
    b = jnp.sum(lp[2:3] * lp[3:4], axis=-1, keepdims=True)
    return jnp.exp(a) - jnp.exp(b) + lam_init


def _diff_attention(q, ks, vs, lam, subln_g, lam_init, keys_transposed=False):
    lane = lax.broadcasted_iota(jnp.int32, q.shape, 1)
    q1 = jnp.where(lane < DQK_A, q, 0.0).astype(BF16)
    q2 = jnp.where(lane >= DQK_A, q, 0.0).astype(BF16)
    kb = [k.astype(BF16) for k in ks]
    scale = DQK_A ** -0.5
    p1 = _softmax_parts([_scores(q1, k, scale, keys_transposed) for k in kb])
    p2 = _softmax_parts([_scores(q2, k, scale, keys_transposed) for k in kb])
    o = None
    for a1, a2, v in zip(p1, p2, vs):
        a = (a1 - lam * a2).astype(BF16)
        t = jnp.dot(a, v.astype(BF16), preferred_element_type=F32)
        o = t if o is None else o + t
    y = o * lax.rsqrt(jnp.mean(o * o, axis=-1, keepdims=True) + EPS) * subln_g
    return y * (1.0 - lam_init)


def _softmax_attention(q, ks, vs, biases):
    qb = q.astype(BF16)
    scale = D_HEAD ** -0.5
    scores = []
    for k, bias in zip(ks, biases):
        s = _scores(qb, k.astype(BF16), scale)
        scores.append(s if bias is None else s + bias)
    ps = _softmax_parts(scores)
    o = None
    for p, v in zip(ps, vs):
        t = jnp.dot(p.astype(BF16), v.astype(BF16), preferred_element_type=F32)
        o = t if o is None else o + t
    return o


def _rope_a(x, cos, sin_lo, sin_hi):
    half = DQK_A // 4
    return x * cos + pltpu.roll(x, LANES - half, 1) * sin_lo + pltpu.roll(x, half, 1) * sin_hi


def _ctx_ab_kernel(qa_ref, ka_ref, va_ref, qb_ref, kb_ref, vb_ref, lp_ref, g_ref,
                   oa_ref, ob_ref, akt_ref, av_ref, bk_ref, bv_ref, *, lam_init, heads):
    lam = _diff_lambda(lp_ref, lam_init)
    seq = qa_ref.shape[0]
    akt_ref[0] = ka_ref[...].T
    av_ref[...] = va_ref[...].reshape(seq, heads, D_HEAD)
    bk_ref[...] = kb_ref[...].reshape(seq, heads, D_HEAD)
    bv_ref[...] = vb_ref[...].reshape(seq, heads, D_HEAD)
    for h in range(heads):
        sl = slice(h * D_HEAD, (h + 1) * D_HEAD)
        oa = _diff_attention(qa_ref[:, sl], [akt_ref[0, sl, :]], [va_ref[:, sl]], lam, g_ref[...], lam_init,
                             keys_transposed=True)
        ob = _softmax_attention(qb_ref[:, sl], [kb_ref[:, sl]], [vb_ref[:, sl]], [None])
        oa_ref[:, sl] = oa.astype(oa_ref.dtype)
        ob_ref[:, sl] = ob.astype(ob_ref.dtype)


def ctx_ab_attention(p, seq, n_heads, diff_lambda, subln_g, lam_init, heads=8):
    m = p.shape[0]
    nb = m // seq
    nhb = n_heads // heads
    w = heads * D_HEAD
    blk = lambda part: pl.BlockSpec((seq, w), lambda b, h, part=part: (b, part * nhb + h))
    out2d = pl.BlockSpec((seq, w), lambda b, h: (b, h))
    cache = pl.BlockSpec((seq, heads, D_HEAD), lambda b, h: (b, h, 0))
    return pl.pallas_call(
        functools.partial(_ctx_ab_kernel, lam_init=lam_init, heads=heads),
        grid=(nb, nhb),
        in_specs=[blk(0), blk(1), blk(2), blk(3), blk(4), blk(5),
                  pl.BlockSpec((4, DQK_A), lambda b, h: (0, 0)),
                  pl.BlockSpec((1, D_HEAD), lambda b, h: (0, 0))],
        out_specs=[out2d, out2d, pl.BlockSpec((1, w, seq), lambda b, h: (b, h, 0)), cache, cache, cache],
        out_shape=[jax.ShapeDtypeStruct((m, n_heads * D_HEAD), BF16),
                   jax.ShapeDtypeStruct((m, n_heads * D_HEAD), BF16),
                   jax.ShapeDtypeStruct((nb, n_heads * D_HEAD, seq), F32),
                   jax.ShapeDtypeStruct((m, n_heads, D_HEAD), F32),
                   jax.ShapeDtypeStruct((m, n_heads, D_HEAD), F32),
                   jax.ShapeDtypeStruct((m, n_heads, D_HEAD), F32)],
        compiler_params=_params("parallel", "parallel"),
        name="ctx_ab_attention",
    )(p, p, p, p, p, p, diff_lambda, subln_g.reshape(1, D_HEAD))


def _na_bias_kernel(rpb_ref, o_ref, *, n_rows):
    h = pl.program_id(0)
    nd = 2 * WIN_C - 1
    ne = 2 * WIN_R - 1
    shape = (GRID_W, LANES)
    lane = lax.broadcasted_iota(jnp.int32, shape, 1)
    qc = lax.broadcasted_iota(jnp.int32, shape, 0)
    kc = lane & (GRID_W - 1)
    diff = kc - qc + (WIN_C - 1)
    cs = jnp.clip(qc - WIN_C // 2, 0, GRID_W - WIN_C)
    col_ok = (kc >= cs) & (kc < cs + WIN_C)
    low_half = lane < GRID_W
    neg = jnp.full(shape, -jnp.inf, F32)

    toeplitz = []
    for e in range(ne):
        acc = jnp.zeros(shape, F32)
        for d in range(nd):
            acc = jnp.where(diff == d, rpb_ref[h, e * nd + d], acc)
        toeplitz.append(jnp.where(col_ok, acc, neg))

    wr = min(WIN_R, n_rows)
    for r in range(n_rows):
        rs = min(max(r - wr // 2, 0), n_rows - wr)
        for pair in range(n_rows // 2):
            kr0, kr1 = 2 * pair, 2 * pair + 1
            a = toeplitz[kr0 - r + WIN_R - 1] if rs <= kr0 < rs + wr else neg
            b = toeplitz[kr1 - r + WIN_R - 1] if rs <= kr1 < rs + wr else neg
            blk = a if a is b else jnp.where(low_half, a, b)
            o_ref[0, r * GRID_W:(r + 1) * GRID_W, pair * LANES:(pair + 1) * LANES] = blk


def na_bias_table(rpb, n_rows):
    n_heads = rpb.shape[0]
    t = n_rows * GRID_W
    return pl.pallas_call(
        functools.partial(_na_bias_kernel, n_rows=n_rows),
        grid=(n_heads,),
        in_specs=[pl.BlockSpec(memory_space=pltpu.SMEM)],
        out_specs=pl.BlockSpec((1, t, t), lambda h: (h, 0, 0)),
        out_shape=jax.ShapeDtypeStruct((n_heads, t, t), F32),
        compiler_params=_params("parallel"),
        name="na_bias_table",
    )(rpb.reshape(n_heads, -1))


def _lat_ab_kernel(qa_ref, ka_ref, va_ref, qb_ref, kb_ref, vb_ref,
                   cak_ref, cav_ref, cbk_ref, cbv_ref, bias_ref,
                   qcos_ref, qslo_ref, qshi_ref, kcos_ref, kslo_ref, kshi_ref,
                   lp_ref, g_ref, oa_ref, ob_ref, *, lam_init):
    lam = _diff_lambda(lp_ref, lam_init)
    q = _rope_a(qa_ref[...], qcos_ref[...], qslo_ref[...], qshi_ref[...])
    k = _rope_a(ka_ref[...], kcos_ref[...], kslo_ref[...], kshi_ref[...])
    oa = _diff_attention(q, [k, cak_ref[0]], [va_ref[...], cav_ref[0]], lam, g_ref[...], lam_init)
    ob = _softmax_attention(qb_ref[...], [kb_ref[...], cbk_ref[0]], [vb_ref[...], cbv_ref[0]],
                            [bias_ref[0], None])
    oa_ref[...] = oa.astype(oa_ref.dtype)
    ob_ref[...] = ob.astype(ob_ref.dtype)


def lat_ab_attention(p, seq, n_heads, cache_a_k, cache_a_v, cache_b_k, cache_b_v, bias, rope,
                     diff_lambda, subln_g, lam_init, tq=256):
    m = p.shape[0]
    nb = m // seq
    nq = seq // tq
    past = cache_a_k.shape[1]
    qblk = lambda part: pl.BlockSpec((tq, D_HEAD), lambda b, h, i, part=part: (b * nq + i, part * n_heads + h))
    kblk = lambda part: pl.BlockSpec((seq, D_HEAD), lambda b, h, i, part=part: (b, part * n_heads + h))
    cblk = pl.BlockSpec((1, past, D_HEAD), lambda b, h, i: (b, 0, h))
    qtab = pl.BlockSpec((tq, D_HEAD), lambda b, h, i: (i, 0))
    ktab = pl.BlockSpec((seq, D_HEAD), lambda b, h, i: (0, 0))
    cos, sin_lo, sin_hi = rope
    return pl.pallas_call(
        functools.partial(_lat_ab_kernel, lam_init=lam_init),
        grid=(nb, n_heads, nq),
        in_specs=[qblk(0), kblk(1), kblk(2), qblk(3), kblk(4), kblk(5),
                  cblk, cblk, cblk, cblk,
                  pl.BlockSpec((1, tq, seq), lambda b, h, i: (h, i, 0)),
                  qtab, qtab, qtab, ktab, ktab, ktab,
                  pl.BlockSpec((4, DQK_A), lambda b, h, i: (0, 0)),
                  pl.BlockSpec((1, D_HEAD), lambda b, h, i: (0, 0))],
        out_specs=[pl.BlockSpec((tq, D_HEAD), lambda b, h, i: (b * nq + i, h))] * 2,
        out_shape=[jax.ShapeDtypeStruct((m, n_heads * D_HEAD), BF16)] * 2,
        compiler_params=_params("parallel", "parallel", "parallel"),
        name="lat_ab_attention",
    )(p, p, p, p, p, p, cache_a_k, cache_a_v, cache_b_k, cache_b_v, bias,
      cos, sin_lo, sin_hi, cos, sin_lo, sin_hi, diff_lambda, subln_g.reshape(1, D_HEAD))


def _decay_mask(row0, tq, tk, lg_f, lg_b):
    i = lax.broadcasted_iota(jnp.int32, (tq, tk), 0) + row0
    j = lax.broadcasted_iota(jnp.int32, (tq, tk), 1)
    rel = (i - j).astype(F32)
    fwd = jnp.where(rel >= 0.0, jnp.exp(lg_f * jnp.maximum(rel, 0.0)), 0.0)
    bwd = jnp.where(rel <= 0.0, jnp.exp(lg_b * jnp.maximum(-rel, 0.0)), 0.0)
    return fwd + bwd


def _gated_head_norm(o, g, norm_g):
    y = o * lax.rsqrt(jnp.mean(o * o, axis=-1, keepdims=True) + EPS) * norm_g
    return _silu(g.astype(F32)) * y


def _ctx_ret_kernel(lg_ref, q_ref, k_ref, v_ref, g_ref, ng_ref, o_ref, sf_ref, sb_ref, *, heads):
    t = q_ref.shape[0]
    pos = lax.broadcasted_iota(jnp.int32, (t, 1), 0).astype(F32)
    for hh in range(heads):
        h = pl.program_id(1) * heads + hh
        sl = slice(hh * DK_C, (hh + 1) * DK_C)
        lg_f = lg_ref[0, h]
        lg_b = lg_ref[1, h]
        k = k_ref[:, sl] * (DK_C ** -0.5)
        v = v_ref[:, sl]
        s = lax.dot_general(q_ref[:, sl], k, NT_DIMS, preferred_element_type=F32)
        att = (s * _decay_mask(0, t, t, lg_f, lg_b)).astype(BF16)
        o = jnp.dot(att, v, preferred_element_type=F32)
        o_ref[:, sl] = _gated_head_norm(o, g_ref[:, sl], ng_ref[...]).astype(o_ref.dtype)
        kf32 = k.astype(F32)
        kf = (kf32 * jnp.exp(lg_f * (t - 1.0 - pos))).T.astype(BF16)
        kb = (kf32 * jnp.exp(lg_b * pos)).T.astype(BF16)
        sf_ref[hh] = jnp.dot(kf, v, preferred_element_type=F32)
        sb_ref[hh] = jnp.dot(kb, v, preferred_element_type=F32)


def ctx_retention(p, seq, n_heads, lg, norm_g, heads=4):
    m = p.shape[0]
    nb = m // seq
    nhb = n_heads // heads
    w = heads * DK_C
    blk = lambda part: pl.BlockSpec((seq, w), lambda b, h, part=part: (b, part * nhb + h))
    state = pl.BlockSpec((None, None, heads, DK_C, DK_C), lambda b, h: (b, 0, h, 0, 0))
    return pl.pallas_call(
        functools.partial(_ctx_ret_kernel, heads=heads),
        grid=(nb, nhb),
        in_specs=[pl.BlockSpec(memory_space=pltpu.SMEM), blk(0), blk(1), blk(2), blk(3),
                  pl.BlockSpec((1, DK_C), lambda b, h: (0, 0))],
        out_specs=[pl.BlockSpec((seq, w), lambda b, h: (b, h)), state, state],
        out_shape=[jax.ShapeDtypeStruct((m, n_heads * DK_C), BF16),
                   jax.ShapeDtypeStruct((nb, 1, n_heads, DK_C, DK_C), F32),
                   jax.ShapeDtypeStruct((nb, 1, n_heads, DK_C, DK_C), F32)],
        compiler_params=_params("parallel", "parallel"),
        name="ctx_retention",
    )(lg, p, p, p, p, norm_g.reshape(1, DK_C))


def _rope_c(x, cos, sin):
    halves = []
    for c in range(DK_C // LANES):
        sl = slice(c * LANES, (c + 1) * LANES)
        xs = x[:, sl]
        halves.append(xs * cos[:, sl] + pltpu.roll(xs, LANES // 2, 1) * sin[:, sl])
    return jnp.concatenate(halves, axis=-1)


def _lat_ret_kernel(lg_ref, q_ref, k_ref, v_ref, g_ref, s0f_ref, s0b_ref, ng_ref,
                    qcos_ref, qsin_ref, kcos_ref, ksin_ref, o_ref, *, tq):
    h = pl.program_id(1)
    row0 = pl.program_id(2) * tq
    lg_f = lg_ref[0, h]
    lg_b = lg_ref[1, h]
    t = k_ref.shape[0]
    q = _rope_c(q_ref[...].astype(F32), qcos_ref[...], qsin_ref[...]).astype(BF16)
    k = _rope_c(k_ref[...].astype(F32) * (DK_C ** -0.5), kcos_ref[...], ksin_ref[...]).astype(BF16)
    s = lax.dot_general(q, k, NT_DIMS, preferred_element_type=F32)
    att = (s * _decay_mask(row0, tq, t, lg_f, lg_b)).astype(BF16)
    o = jnp.dot(att, v_ref[...], preferred_element_type=F32)
    pos = (lax.broadcasted_iota(jnp.int32, (tq, 1), 0) + row0).astype(F32)
    o += jnp.dot(q, s0f_ref[...].astype(BF16), preferred_element_type=F32) * jnp.exp(lg_f * (pos + 1.0))
    o += jnp.dot(q, s0b_ref[...].astype(BF16), preferred_element_type=F32) * jnp.exp(lg_b * (t - pos))
    o_ref[...] = _gated_head_norm(o, g_ref[...], ng_ref[...]).astype(o_ref.dtype)


def lat_retention(p, seq, n_heads, lg, norm_g, s0_f, s0_b, rope, tq=256):
    m = p.shape[0]
    nb = m // seq
    nq = seq // tq
    qblk = lambda part: pl.BlockSpec((tq, DK_C), lambda b, h, i, part=part: (b * nq + i, part * n_heads + h))
    kblk = lambda part: pl.BlockSpec((seq, DK_C), lambda b, h, i, part=part: (b, part * n_heads + h))
    state = pl.BlockSpec((None, None, None, DK_C, DK_C), lambda b, h, i: (b, 0, h, 0, 0))
    qtab = pl.BlockSpec((tq, DK_C), lambda b, h, i: (i, 0))
    ktab = pl.BlockSpec((seq, DK_C), lambda b, h, i: (0, 0))
    cos, sin = rope
    return pl.pallas_call(
        functools.partial(_lat_ret_kernel, tq=tq),
        grid=(nb, n_heads, nq),
        in_specs=[pl.BlockSpec(memory_space=pltpu.SMEM), qblk(0), kblk(1), kblk(2), qblk(3),
                  state, state, pl.BlockSpec((1, DK_C), lambda b, h, i: (0, 0)),
                  qtab, qtab, ktab, ktab],
        out_specs=pl.BlockSpec((tq, DK_C), lambda b, h, i: (b * nq + i, h)),
        out_shape=jax.ShapeDtypeStruct((m, n_heads * DK_C), BF16),
        compiler_params=_params("parallel", "parallel", "parallel"),
        name="lat_retention",
    )(lg, p, p, p, p, s0_f, s0_b, norm_g.reshape(1, DK_C), cos, sin, cos, sin)


def _rope_tables(seq, width, group):
    half = group // 2
    t = np.arange(seq)
    rows, cols = t // GRID_W, t % GRID_W
    lane = np.arange(width)
    inv = (ROPE_BASE ** (-jnp.arange(half, dtype=F32) / half))[lane % half]
    use_cols = (lane // group) % 2 == 1
    pos = np.where(use_cols[None, :], cols[:, None], rows[:, None])
    ang = jnp.asarray(pos, dtype=jnp.int32).astype(F32) * inv[None, :]
    first = jnp.asarray((lane % group) < half)
    return jnp.cos(ang), jnp.sin(ang), first


def rope_tables_a(seq):
    cos, sin, first = _rope_tables(seq, D_HEAD, DQK_A // 2)
    return cos, jnp.where(first, -sin, 0.0), jnp.where(first, 0.0, sin)


def rope_tables_c(seq):
    cos, sin, first = _rope_tables(seq, DK_C, DK_C // 2)
    return cos, jnp.where(first, -sin, sin)


def kernel(x_prompt, x_sample, c, cache_a_k, cache_a_v, cache_b_k, cache_b_v, state_c_fwd, state_c_bwd,
           c_ctx, norm_mix_g, norm_mlp_g, w_mod, b_mod, w_mlp_in, w_mlp_out, w_in_ab, w_out_ab,
           diff_lambda, diff_subln_g, nat_rpb, w_in_c, w_out_c, ret_log_decay_fwd, ret_log_decay_bwd,
           ret_norm_g, final_norm_g):
    bp, seq_p, d = x_prompt.shape
    bs, seq_s, _ = x_sample.shape
    depth = w_mod.shape[0]
    n_heads_ab = w_out_ab.shape[1] // (2 * D_HEAD)
    n_heads_c = w_out_c.shape[1] // DK_C
    past = cache_a_k.shape[2]
    w_ab = n_heads_ab * D_HEAD

    xp = x_prompt.reshape(bp * seq_p, d)
    xs = x_sample.reshape(bs * seq_s, d)
    rows_p = bp * seq_p

    cvec = jnp.concatenate([c_ctx[None, :], c, jnp.zeros((MOD_ROWS - 1 - bs, d), F32)], axis=0)
    mod = mod_vectors(cvec, w_mod, b_mod).reshape(depth, MOD_ROWS, 6, 1, d)

    new_a_k, new_a_v, new_b_k, new_b_v, new_c_f, new_c_b = [], [], [], [], [], []

    for l in range(depth):
        sh1_p, sc1_p, g1_p, sh2_p, sc2_p, g2_p = (mod[l, 0:1, i] for i in range(6))
        sh1_s, sc1_s, g1_s, sh2_s, sc2_s, g2_s = (mod[l, 1:1 + bs, i] for i in range(6))
        hp = norm_modulate(xp, norm_mix_g[l], sh1_p, sc1_p, rows_p)
        hs = norm_modulate(xs, norm_mix_g[l], sh1_s, sc1_s, seq_s)

        if l % 2 == 0:
            e = l // 2
            lam_init = 0.8 - 0.6 * float(np.exp(-0.3 * l))
            w_out, w_out_layer = w_out_ab, e
            pp = matmul(hp, w_in_ab, e)
            ps = matmul(hs, w_in_ab, e)
            oa, ob, akt, av, bk, bv = ctx_ab_attention(pp, seq_p, n_heads_ab, diff_lambda[e],
                                                       diff_subln_g[e], lam_init)
            mix_p = [oa, ob]
            new_a_k.append(akt.reshape(bp, 2 * n_heads_ab, DQK_A, seq_p).transpose(0, 3, 1, 2))
            new_a_v.append(av.reshape(bp, seq_p, n_heads_ab, D_HEAD))
            new_b_k.append(bk.reshape(bp, seq_p, n_heads_ab, D_HEAD))
            new_b_v.append(bv.reshape(bp, seq_p, n_heads_ab, D_HEAD))
            bias = na_bias_table(nat_rpb[e], seq_s // GRID_W)
            mix_s = lat_ab_attention(
                ps, seq_s, n_heads_ab,
                cache_a_k[:, e].reshape(bs, past, w_ab), cache_a_v[:, e].reshape(bs, past, w_ab),
                cache_b_k[:, e].reshape(bs, past, w_ab), cache_b_v[:, e].reshape(bs, past, w_ab),
                bias, rope_tables_a(seq_s), diff_lambda[e], diff_subln_g[e], lam_init)
        else:
            o = l // 2
            w_out, w_out_layer = w_out_c, o
            lg = jnp.stack([ret_log_decay_fwd[o], ret_log_decay_bwd[o]])
            pp = matmul(hp, w_in_c, o, out_dtype=BF16)
            ps = matmul(hs, w_in_c, o, out_dtype=BF16)
            mix_p, s_f, s_b = ctx_retention(pp, seq_p, n_heads_c, lg, ret_norm_g[o])
            new_c_f.append(s_f)
            new_c_b.append(s_b)
            mix_s = lat_retention(ps, seq_s, n_heads_c, lg, ret_norm_g[o],
                                  state_c_fwd[:, o:o + 1], state_c_bwd[:, o:o + 1], rope_tables_c(seq_s))

        xp = matmul(mix_p, w_out, w_out_layer, mode="resgate", res=xp, gate=g1_p, rows_per_group=rows_p)
        xs = matmul(mix_s, w_out, w_out_layer, mode="resgate", res=xs, gate=g1_s, rows_per_group=seq_s)

        hp = norm_modulate(xp, norm_mlp_g[l], sh2_p, sc2_p, rows_p)
        hs = norm_modulate(xs, norm_mlp_g[l], sh2_s, sc2_s, seq_s)
        up = matmul(hp, w_mlp_in, l, mode="relu2", out_dtype=BF16)
        us = matmul(hs, w_mlp_in, l, mode="relu2", out_dtype=BF16)
        xp = matmul(up, w_mlp_out, l, mode="resgate", res=xp, gate=g2_p, rows_per_group=rows_p)
        xs = matmul(us, w_mlp_out, l, mode="resgate", res=xs, gate=g2_s, rows_per_group=seq_s)

    y_prompt = rms_norm(xp, final_norm_g).reshape(bp, seq_p, d)
    y_sample = rms_norm(xs, final_norm_g).reshape(bs, seq_s, d)
    return (y_prompt, y_sample,
            jnp.stack(new_a_k, axis=1), jnp.stack(new_a_v, axis=1),
            jnp.stack(new_b_k, axis=1), jnp.stack(new_b_v, axis=1),
            jnp.concatenate(new_c_f, axis=1), jnp.concatenate(new_c_b, axis=1))
```

```python
import functools

import jax
import jax.numpy as jnp
import numpy as np
from jax import lax
from jax.experimental import pallas as pl
from jax.experimental.pallas import tpu as pltpu

F32 = jnp.float32
BF16 = jnp.bfloat16

EPS = 1e-6
ROPE_BASE = 10000.0
GRID_W = 64
WIN_R = 8
WIN_C = 16
DQK_A = 64
D_HEAD = 128
DK_C = 256
LANES = 128
MOD_ROWS = 8
VMEM_LIMIT = 56 * 1024 * 1024

NT_DIMS = (((1,), (1,)), ((), ()))


def _params(*sem):
    return pltpu.CompilerParams(dimension_semantics=sem, vmem_limit_bytes=VMEM_LIMIT)


def _silu(x):
    return x / (1.0 + jnp.exp(-x))


def _modvec_kernel(c_ref, w_ref, b_ref, o_ref):
    s = _silu(c_ref[...]).astype(BF16)
    w = w_ref[0].astype(BF16)
    o_ref[0] = jnp.dot(s, w, preferred_element_type=F32) + b_ref[0]


def mod_vectors(cvec, w_mod, b_mod, tn=512):
    depth, d, n = w_mod.shape
    return pl.pallas_call(
        _modvec_kernel,
        grid=(depth, n // tn),
        in_specs=[
            pl.BlockSpec((MOD_ROWS, d), lambda l, j: (0, 0)),
            pl.BlockSpec((1, d, tn), lambda l, j: (l, 0, j)),
            pl.BlockSpec((1, 1, tn), lambda l, j: (l, 0, j)),
        ],
        out_specs=pl.BlockSpec((1, MOD_ROWS, tn), lambda l, j: (l, 0, j)),
        out_shape=jax.ShapeDtypeStruct((depth, MOD_ROWS, n), F32),
        compiler_params=_params("parallel", "parallel"),
        name="mod_vectors",
    )(cvec, w_mod, b_mod.reshape(depth, 1, n))


def _norm_mod_kernel(x_ref, g_ref, sh_ref, sc_ref, o_ref):
    x = x_ref[...]
    y = x * lax.rsqrt(jnp.mean(x * x, axis=-1, keepdims=True) + EPS) * g_ref[...]
    o_ref[...] = (y * (1.0 + sc_ref[0]) + sh_ref[0]).astype(o_ref.dtype)


def norm_modulate(x, g, shift, scale, rows_per_group, tr=256):
    m, d = x.shape
    grp = lambda i: ((i * tr) // rows_per_group, 0, 0)
    return pl.pallas_call(
        _norm_mod_kernel,
        grid=(m // tr,),
        in_specs=[
            pl.BlockSpec((tr, d), lambda i: (i, 0)),
            pl.BlockSpec((1, d), lambda i: (0, 0)),
            pl.BlockSpec((1, 1, d), grp),
            pl.BlockSpec((1, 1, d), grp),
        ],
        out_specs=pl.BlockSpec((tr, d), lambda i: (i, 0)),
        out_shape=jax.ShapeDtypeStruct((m, d), BF16),
        compiler_params=_params("parallel"),
        name="norm_modulate",
    )(x, g.reshape(1, d), shift, scale)


def _norm_kernel(x_ref, g_ref, o_ref):
    x = x_ref[...]
    o_ref[...] = x * lax.rsqrt(jnp.mean(x * x, axis=-1, keepdims=True) + EPS) * g_ref[...]


def rms_norm(x, g, tr=256):
    m, d = x.shape
    return pl.pallas_call(
        _norm_kernel,
        grid=(m // tr,),
        in_specs=[pl.BlockSpec((tr, d), lambda i: (i, 0)), pl.BlockSpec((1, d), lambda i: (0, 0))],
        out_specs=pl.BlockSpec((tr, d), lambda i: (i, 0)),
        out_shape=jax.ShapeDtypeStruct((m, d), F32),
        compiler_params=_params("parallel"),
        name="rms_norm",
    )(x, g.reshape(1, d))


def _mm_kernel(*refs, mode, nk, n_pairs, group_rows):
    x_refs = refs[:n_pairs]
    w_refs = refs[n_pairs:2 * n_pairs]
    refs = refs[2 * n_pairs:]
    if mode == "resgate":
        res_ref, gate_ref, o_ref = refs
    else:
        (o_ref,) = refs

    def product():
        acc = None
        for x_ref, w_ref in zip(x_refs, w_refs):
            t = jnp.dot(x_ref[...], w_ref[...].astype(BF16), preferred_element_type=F32)
            acc = t if acc is None else acc + t
        return acc

    def epilogue(acc):
        if mode == "relu2":
            r = jnp.maximum(acc, 0.0)
            o_ref[...] = (r * r).astype(o_ref.dtype)
        elif mode == "resgate":
            for g in range(gate_ref.shape[0]):
                rows = slice(g * group_rows, (g + 1) * group_rows)
                o_ref[rows, :] = res_ref[rows, :] + gate_ref[g] * acc[rows, :]
        else:
            o_ref[...] = acc.astype(o_ref.dtype)

    if nk == 1:
        epilogue(product())
        return

    k = pl.program_id(2)

    @pl.when(k == 0)
    def _():
        o_ref[...] = product()

    @pl.when(jnp.logical_and(k > 0, k < nk - 1))
    def _():
        o_ref[...] += product()

    @pl.when(k == nk - 1)
    def _():
        epilogue(o_ref[...] + product())


def matmul(xs, w, layer, *, mode="plain", out_dtype=F32, res=None, gate=None, rows_per_group=None,
           tm=2048, tn=512, tk=4096):
    xs = list(xs) if isinstance(xs, (list, tuple)) else [xs]
    n_pairs = len(xs)
    m, kpiece = xs[0].shape
    n = w.shape[2]
    tm = min(tm, m)
    tk = min(tk, kpiece)
    nk = kpiece // tk
    assert n_pairs == 1 or nk == 1
    assert nk == 1 or out_dtype == F32
    x_mode = dict(pipeline_mode=pl.Buffered(1)) if nk == 1 else {}
    in_specs = [pl.BlockSpec((tm, tk), lambda i, j, k: (i, k), **x_mode) for _ in xs]
    in_specs += [pl.BlockSpec((None, tk, tn), lambda i, j, k, piece=piece: (layer, k + piece, j))
                 for piece in range(n_pairs)]
    args = xs + [w] * n_pairs
    group_rows = None
    if mode == "resgate":
        group_rows = min(rows_per_group, tm)
        groups_per_tile = tm // group_rows
        tiles_per_group = rows_per_group // group_rows
        in_specs += [
            pl.BlockSpec((tm, tn), lambda i, j, k: (i, j)),
            pl.BlockSpec((groups_per_tile, 1, tn), lambda i, j, k: (i // tiles_per_group, 0, j)),
        ]
        args += [res, gate]
    return pl.pallas_call(
        functools.partial(_mm_kernel, mode=mode, nk=nk, n_pairs=n_pairs, group_rows=group_rows),
        grid=(m // tm, n // tn, nk),
        in_specs=in_specs,
        out_specs=pl.BlockSpec((tm, tn), lambda i, j, k: (i, j)),
        out_shape=jax.ShapeDtypeStruct((m, n), out_dtype),
        compiler_params=_params("parallel", "parallel", "arbitrary"),
        name="matmul_" + mode,
    )(*args)


def _softmax_parts(scores):
    m = functools.reduce(jnp.maximum, [jnp.max(s, axis=-1, keepdims=True) for s in scores])
    es = [jnp.exp(s - m) for s in scores]
    den = functools.reduce(jnp.add, [jnp.sum(e, axis=-1, keepdims=True) for e in es])
    inv = 1.0 / den
    return [e * inv for e in es]


def _scores(q_bf, k_bf, scale, keys_transposed=False):
    if keys_transposed:
        return jnp.dot(q_bf, k_bf, preferred_element_type=F32) * scale
    return lax.dot_general(q_bf, k_bf, NT_DIMS, preferred_element_type=F32) * scale


def _diff_lambda(lp_ref, lam_init):
    lp = lp_ref[...]
    a = jnp.sum(lp[0:1] * lp[1:2], axis=-1, keepdims=True)
    b = jnp.sum(lp[2:3] * lp[3:4], axis=-1, keepdims=True)
    return jnp.exp(a) - jnp.exp(b) + lam_init


def _diff_attention(q, ks, vs, lam, subln_g, lam_init, keys_transposed=False):
    lane = lax.broadcasted_iota(jnp.int32, q.shape, 1)
    q1 = jnp.where(lane < DQK_A, q, 0.0).astype(BF16)
    q2 = jnp.where(lane >= DQK_A, q, 0.0).astype(BF16)
    kb = [k.astype(BF16) for k in ks]
    scale = DQK_A ** -0.5
    p1 = _softmax_parts([_scores(q1, k, scale, keys_transposed) for k in kb])
    p2 = _softmax_parts([_scores(q2, k, scale, keys_transposed) for k in kb])
    o = None
    for a1, a2, v in zip(p1, p2, vs):
        a = (a1 - lam * a2).astype(BF16)
        t = jnp.dot(a, v.astype(BF16), preferred_element_type=F32)
        o = t if o is None else o + t
    y = o * lax.rsqrt(jnp.mean(o * o, axis=-1, keepdims=True) + EPS) * subln_g
    return y * (1.0 - lam_init)


def _softmax_attention(q, ks, vs, biases):
    qb = q.astype(BF16)
    scale = D_HEAD ** -0.5
    scores = []
    for k, bias in zip(ks, biases):
        s = _scores(qb, k.astype(BF16), scale)
        scores.append(s if bias is None else s + bias)
    ps = _softmax_parts(scores)
    o = None
    for p, v in zip(ps, vs):
        t = jnp.dot(p.astype(BF16), v.astype(BF16), preferred_element_type=F32)
        o = t if o is None else o + t
    return o


def _rope_a(x, cos, sin_lo, sin_hi):
    half = DQK_A // 4
    return x * cos + pltpu.roll(x, LANES - half, 1) * sin_lo + pltpu.roll(x, half, 1) * sin_hi


def _ctx_ab_kernel(qa_ref, ka_ref, va_ref, qb_ref, kb_ref, vb_ref, lp_ref, g_ref,
                   oa_ref, ob_ref, akt_ref, av_ref, bk_ref, bv_ref, *, lam_init, heads):
    lam = _diff_lambda(lp_ref, lam_init)
    seq = qa_ref.shape[0]
    akt_ref[0] = ka_ref[...].T
    av_ref[...] = va_ref[...].reshape(seq, heads, D_HEAD)
    bk_ref[...] = kb_ref[...].reshape(seq, heads, D_HEAD)
    bv_ref[...] = vb_ref[...].reshape(seq, heads, D_HEAD)
    for h in range(heads):
        sl = slice(h * D_HEAD, (h + 1) * D_HEAD)
        oa = _diff_attention(qa_ref[:, sl], [akt_ref[0, sl, :]], [va_ref[:, sl]], lam, g_ref[...], lam_init,
                             keys_transposed=True)
        ob = _softmax_attention(qb_ref[:, sl], [kb_ref[:, sl]], [vb_ref[:, sl]], [None])
        oa_ref[:, sl] = oa.astype(oa_ref.dtype)
        ob_ref[:, sl] = ob.astype(ob_ref.dtype)


def ctx_ab_attention(p, seq, n_heads, diff_lambda, subln_g, lam_init, heads=8):
    m = p.shape[0]
    nb = m // seq
    nhb = n_heads // heads
    w = heads * D_HEAD
    blk = lambda part: pl.BlockSpec((seq, w), lambda b, h, part=part: (b, part * nhb + h))
    out2d = pl.BlockSpec((seq, w), lambda b, h: (b, h))
    cache = pl.BlockSpec((seq, heads, D_HEAD), lambda b, h: (b, h, 0))
    return pl.pallas_call(
        functools.partial(_ctx_ab_kernel, lam_init=lam_init, heads=heads),
        grid=(nb, nhb),
        in_specs=[blk(0), blk(1), blk(2), blk(3), blk(4), blk(5),
                  pl.BlockSpec((4, DQK_A), lambda b, h: (0, 0)),
                  pl.BlockSpec((1, D_HEAD), lambda b, h: (0, 0))],
        out_specs=[out2d, out2d, pl.BlockSpec((1, w, seq), lambda b, h: (b, h, 0)), cache, cache, cache],
        out_shape=[jax.ShapeDtypeStruct((m, n_heads * D_HEAD), BF16),
                   jax.ShapeDtypeStruct((m, n_heads * D_HEAD), BF16),
                   jax.ShapeDtypeStruct((nb, n_heads * D_HEAD, seq), F32),
                   jax.ShapeDtypeStruct((m, n_heads, D_HEAD), F32),
                   jax.ShapeDtypeStruct((m, n_heads, D_HEAD), F32),
                   jax.ShapeDtypeStruct((m, n_heads, D_HEAD), F32)],
        compiler_params=_params("parallel", "parallel"),
        name="ctx_ab_attention",
    )(p, p, p, p, p, p, diff_lambda, subln_g.reshape(1, D_HEAD))


def _na_bias_kernel(rpb_ref, o_ref, *, n_rows):
    h = pl.program_id(0)
    nd = 2 * WIN_C - 1
    ne = 2 * WIN_R - 1
    shape = (GRID_W, LANES)
    lane = lax.broadcasted_iota(jnp.int32, shape, 1)
    qc = lax.broadcasted_iota(jnp.int32, shape, 0)
    kc = lane & (GRID_W - 1)
    diff = kc - qc + (WIN_C - 1)
    cs = jnp.clip(qc - WIN_C // 2, 0, GRID_W - WIN_C)
    col_ok = (kc >= cs) & (kc < cs + WIN_C)
    low_half = lane < GRID_W
    neg = jnp.full(shape, -jnp.inf, F32)

    toeplitz = []
    for e in range(ne):
        acc = jnp.zeros(shape, F32)
        for d in range(nd):
            acc = jnp.where(diff == d, rpb_ref[h, e * nd + d], acc)
        toeplitz.append(jnp.where(col_ok, acc, neg))

    wr = min(WIN_R, n_rows)
    for r in range(n_rows):
        rs = min(max(r - wr // 2, 0), n_rows - wr)
        for pair in range(n_rows // 2):
            kr0, kr1 = 2 * pair, 2 * pair + 1
            a = toeplitz[kr0 - r + WIN_R - 1] if rs <= kr0 < rs + wr else neg
            b = toeplitz[kr1 - r + WIN_R - 1] if rs <= kr1 < rs + wr else neg
            blk = a if a is b else jnp.where(low_half, a, b)
            o_ref[0, r * GRID_W:(r + 1) * GRID_W, pair * LANES:(pair + 1) * LANES] = blk


def na_bias_table(rpb, n_rows):
    n_heads = rpb.shape[0]
    t = n_rows * GRID_W
    return pl.pallas_call(
        functools.partial(_na_bias_kernel, n_rows=n_rows),
        grid=(n_heads,),
        in_specs=[pl.BlockSpec(memory_space=pltpu.SMEM)],
        out_specs=pl.BlockSpec((1, t, t), lambda h: (h, 0, 0)),
        out_shape=jax.ShapeDtypeStruct((n_heads, t, t), F32),
        compiler_params=_params("parallel"),
        name="na_bias_table",
    )(rpb.reshape(n_heads, -1))


def _lat_ab_kernel(qa_ref, ka_ref, va_ref, qb_ref, kb_ref, vb_ref,
                   cak_ref, cav_ref, cbk_ref, cbv_ref, bias_ref,
                   qcos_ref, qslo_ref, qshi_ref, kcos_ref, kslo_ref, kshi_ref,
                   lp_ref, g_ref, oa_ref, ob_ref, *, lam_init):
    lam = _diff_lambda(lp_ref, lam_init)
    q = _rope_a(qa_ref[...], qcos_ref[...], qslo_ref[...], qshi_ref[...])
    k = _rope_a(ka_ref[...], kcos_ref[...], kslo_ref[...], kshi_ref[...])
    oa = _diff_attention(q, [k, cak_ref[0]], [va_ref[...], cav_ref[0]], lam, g_ref[...], lam_init)
    ob = _softmax_attention(qb_ref[...], [kb_ref[...], cbk_ref[0]], [vb_ref[...], cbv_ref[0]],
                            [bias_ref[0], None])
    oa_ref[...] = oa.astype(oa_ref.dtype)
    ob_ref[...] = ob.astype(ob_ref.dtype)


def lat_ab_attention(p, seq, n_heads, cache_a_k, cache_a_v, cache_b_k, cache_b_v, bias, rope,
                     diff_lambda, subln_g, lam_init, tq=256):
    m = p.shape[0]
    nb = m // seq
    nq = seq // tq
    past = cache_a_k.shape[1]
    qblk = lambda part: pl.BlockSpec((tq, D_HEAD), lambda b, h, i, part=part: (b * nq + i, part * n_heads + h))
    kblk = lambda part: pl.BlockSpec((seq, D_HEAD), lambda b, h, i, part=part: (b, part * n_heads + h))
    cblk = pl.BlockSpec((1, past, D_HEAD), lambda b, h, i: (b, 0, h))
    qtab = pl.BlockSpec((tq, D_HEAD), lambda b, h, i: (i, 0))
    ktab = pl.BlockSpec((seq, D_HEAD), lambda b, h, i: (0, 0))
    cos, sin_lo, sin_hi = rope
    return pl.pallas_call(
        functools.partial(_lat_ab_kernel, lam_init=lam_init),
        grid=(nb, n_heads, nq),
        in_specs=[qblk(0), kblk(1), kblk(2), qblk(3), kblk(4), kblk(5),
                  cblk, cblk, cblk, cblk,
                  pl.BlockSpec((1, tq, seq), lambda b, h, i: (h, i, 0)),
                  qtab, qtab, qtab, ktab, ktab, ktab,
                  pl.BlockSpec((4, DQK_A), lambda b, h, i: (0, 0)),
                  pl.BlockSpec((1, D_HEAD), lambda b, h, i: (0, 0))],
        out_specs=[pl.BlockSpec((tq, D_HEAD), lambda b, h, i: (b * nq + i, h))] * 2,
        out_shape=[jax.ShapeDtypeStruct((m, n_heads * D_HEAD), BF16)] * 2,
        compiler_params=_params("parallel", "parallel", "parallel"),
        name="lat_ab_attention",
    )(p, p, p, p, p, p, cache_a_k, cache_a_v, cache_b_k, cache_b_v, bias,
      cos, sin_lo, sin_hi, cos, sin_lo, sin_hi, diff_lambda, subln_g.reshape(1, D_HEAD))


def _decay_mask(row0, tq, tk, lg_f, lg_b):
    i = lax.broadcasted_iota(jnp.int32, (tq, tk), 0) + row0
    j = lax.broadcasted_iota(jnp.int32, (tq, tk), 1)
    rel = (i - j).astype(F32)
    fwd = jnp.where(rel >= 0.0, jnp.exp(lg_f * jnp.maximum(rel, 0.0)), 0.0)
    bwd = jnp.where(rel <= 0.0, jnp.exp(lg_b * jnp.maximum(-rel, 0.0)), 0.0)
    return fwd + bwd


def _gated_head_norm(o, g, norm_g):
    y = o * lax.rsqrt(jnp.mean(o * o, axis=-1, keepdims=True) + EPS) * norm_g
    return _silu(g.astype(F32)) * y


def _ctx_ret_kernel(lg_ref, q_ref, k_ref, v_ref, g_ref, ng_ref, o_ref, sf_ref, sb_ref, *, heads):
    t = q_ref.shape[0]
    pos = lax.broadcasted_iota(jnp.int32, (t, 1), 0).astype(F32)
    for hh in range(heads):
        h = pl.program_id(1) * heads + hh
        sl = slice(hh * DK_C, (hh + 1) * DK_C)
        lg_f = lg_ref[0, h]
        lg_b = lg_ref[1, h]
        k = k_ref[:, sl] * (DK_C ** -0.5)
        v = v_ref[:, sl]
        s = lax.dot_general(q_ref[:, sl], k, NT_DIMS, preferred_element_type=F32)
        att = (s * _decay_mask(0, t, t, lg_f, lg_b)).astype(BF16)
        o = jnp.dot(att, v, preferred_element_type=F32)
        o_ref[:, sl] = _gated_head_norm(o, g_ref[:, sl], ng_ref[...]).astype(o_ref.dtype)
        kf32 = k.astype(F32)
        kf = (kf32 * jnp.exp(lg_f * (t - 1.0 - pos))).T.astype(BF16)
        kb = (kf32 * jnp.exp(lg_b * pos)).T.astype(BF16)
        sf_ref[hh] = jnp.dot(kf, v, preferred_element_type=F32)
        sb_ref[hh] = jnp.dot(kb, v, preferred_element_type=F32)


def ctx_retention(p, seq, n_heads, lg, norm_g, heads=4):
    m = p.shape[0]
    nb = m // seq
    nhb = n_heads // heads
    w = heads * DK_C
    blk = lambda part: pl.BlockSpec((seq, w), lambda b, h, part=part: (b, part * nhb + h))
    state = pl.BlockSpec((None, None, heads, DK_C, DK_C), lambda b, h: (b, 0, h, 0, 0))
    return pl.pallas_call(
        functools.partial(_ctx_ret_kernel, heads=heads),
        grid=(nb, nhb),
        in_specs=[pl.BlockSpec(memory_space=pltpu.SMEM), blk(0), blk(1), blk(2), blk(3),
                  pl.BlockSpec((1, DK_C), lambda b, h: (0, 0))],
        out_specs=[pl.BlockSpec((seq, w), lambda b, h: (b, h)), state, state],
        out_shape=[jax.ShapeDtypeStruct((m, n_heads * DK_C), BF16),
                   jax.ShapeDtypeStruct((nb, 1, n_heads, DK_C, DK_C), F32),
                   jax.ShapeDtypeStruct((nb, 1, n_heads, DK_C, DK_C), F32)],
        compiler_params=_params("parallel", "parallel"),
        name="ctx_retention",
    )(lg, p, p, p, p, norm_g.reshape(1, DK_C))


def _rope_c(x, cos, sin):
    halves = []
    for c in range(DK_C // LANES):
        sl = slice(c * LANES, (c + 1) * LANES)
        xs = x[:, sl]
        halves.append(xs * cos[:, sl] + pltpu.roll(xs, LANES // 2, 1) * sin[:, sl])
    return jnp.concatenate(halves, axis=-1)


def _lat_ret_kernel(lg_ref, q_ref, k_ref, v_ref, g_ref, s0f_ref, s0b_ref, ng_ref,
                    qcos_ref, qsin_ref, kcos_ref, ksin_ref, o_ref, *, tq):
    h = pl.program_id(1)
    row0 = pl.program_id(2) * tq
    lg_f = lg_ref[0, h]
    lg_b = lg_ref[1, h]
    t = k_ref.shape[0]
    q = _rope_c(q_ref[...].astype(F32), qcos_ref[...], qsin_ref[...]).astype(BF16)
    k = _rope_c(k_ref[...].astype(F32) * (DK_C ** -0.5), kcos_ref[...], ksin_ref[...]).astype(BF16)
    s = lax.dot_general(q, k, NT_DIMS, preferred_element_type=F32)
    att = (s * _decay_mask(row0, tq, t, lg_f, lg_b)).astype(BF16)
    o = jnp.dot(att, v_ref[...], preferred_element_type=F32)
    pos = (lax.broadcasted_iota(jnp.int32, (tq, 1), 0) + row0).astype(F32)
    o += jnp.dot(q, s0f_ref[...].astype(BF16), preferred_element_type=F32) * jnp.exp(lg_f * (pos + 1.0))
    o += jnp.dot(q, s0b_ref[...].astype(BF16), preferred_element_type=F32) * jnp.exp(lg_b * (t - pos))
    o_ref[...] = _gated_head_norm(o, g_ref[...], ng_ref[...]).astype(o_ref.dtype)


def lat_retention(p, seq, n_heads, lg, norm_g, s0_f, s0_b, rope, tq=256):
    m = p.shape[0]
    nb = m // seq
    nq = seq // tq
    qblk = lambda part: pl.BlockSpec((tq, DK_C), lambda b, h, i, part=part: (b * nq + i, part * n_heads + h))
    kblk = lambda part: pl.BlockSpec((seq, DK_C), lambda b, h, i, part=part: (b, part * n_heads + h))
    state = pl.BlockSpec((None, None, None, DK_C, DK_C), lambda b, h, i: (b, 0, h, 0, 0))
    qtab = pl.BlockSpec((tq, DK_C), lambda b, h, i: (i, 0))
    ktab = pl.BlockSpec((seq, DK_C), lambda b, h, i: (0, 0))
    cos, sin = rope
    return pl.pallas_call(
        functools.partial(_lat_ret_kernel, tq=tq),
        grid=(nb, n_heads, nq),
        in_specs=[pl.BlockSpec(memory_space=pltpu.SMEM), qblk(0), kblk(1), kblk(2), qblk(3),
                  state, state, pl.BlockSpec((1, DK_C), lambda b, h, i: (0, 0)),
                  qtab, qtab, ktab, ktab],
        out_specs=pl.BlockSpec((tq, DK_C), lambda b, h, i: (b * nq + i, h)),
        out_shape=jax.ShapeDtypeStruct((m, n_heads * DK_C), BF16),
        compiler_params=_params("parallel", "parallel", "parallel"),
        name="lat_retention",
    )(lg, p, p, p, p, s0_f, s0_b, norm_g.reshape(1, DK_C), cos, sin, cos, sin)


def _rope_tables(seq, width, group):
    half = group // 2
    t = np.arange(seq)
    rows, cols = t // GRID_W, t % GRID_W
    lane = np.arange(width)
    inv = (ROPE_BASE ** (-jnp.arange(half, dtype=F32) / half))[lane % half]
    use_cols = (lane // group) % 2 == 1
    pos = np.where(use_cols[None, :], cols[:, None], rows[:, None])
    ang = jnp.asarray(pos, dtype=jnp.int32).astype(F32) * inv[None, :]
    first = jnp.asarray((lane % group) < half)
    return jnp.cos(ang), jnp.sin(ang), first


def rope_tables_a(seq):
    cos, sin, first = _rope_tables(seq, D_HEAD, DQK_A // 2)
    return cos, jnp.where(first, -sin, 0.0), jnp.where(first, 0.0, sin)


def rope_tables_c(seq):
    cos, sin, first = _rope_tables(seq, DK_C, DK_C // 2)
    return cos, jnp.where(first, -sin, sin)


def kernel(x_prompt, x_sample, c, cache_a_k, cache_a_v, cache_b_k, cache_b_v, state_c_fwd, state_c_bwd,
           c_ctx, norm_mix_g, norm_mlp_g, w_mod, b_mod, w_mlp_in, w_mlp_out, w_in_ab, w_out_ab,
           diff_lambda, diff_subln_g, nat_rpb, w_in_c, w_out_c, ret_log_decay_fwd, ret_log_decay_bwd,
           ret_norm_g, final_norm_g):
    bp, seq_p, d = x_prompt.shape
    bs, seq_s, _ = x_sample.shape
    depth = w_mod.shape[0]
    n_heads_ab = w_out_ab.shape[1] // (2 * D_HEAD)
    n_heads_c = w_out_c.shape[1] // DK_C
    past = cache_a_k.shape[2]
    w_ab = n_heads_ab * D_HEAD

    xp = x_prompt.reshape(bp * seq_p, d)
    xs = x_sample.reshape(bs * seq_s, d)
    rows_p = bp * seq_p

    cvec = jnp.concatenate([c_ctx[None, :], c, jnp.zeros((MOD_ROWS - 1 - bs, d), F32)], axis=0)
    mod = mod_vectors(cvec, w_mod, b_mod).reshape(depth, MOD_ROWS, 6, 1, d)

    new_a_k, new_a_v, new_b_k, new_b_v, new_c_f, new_c_b = [], [], [], [], [], []

    for l in range(depth):
        sh1_p, sc1_p, g1_p, sh2_p, sc2_p, g2_p = (mod[l, 0:1, i] for i in range(6))
        sh1_s, sc1_s, g1_s, sh2_s, sc2_s, g2_s = (mod[l, 1:1 + bs, i] for i in range(6))
        hp = norm_modulate(xp, norm_mix_g[l], sh1_p, sc1_p, rows_p)
        hs = norm_modulate(xs, norm_mix_g[l], sh1_s, sc1_s, seq_s)

        if l % 2 == 0:
            e = l // 2
            lam_init = 0.8 - 0.6 * float(np.exp(-0.3 * l))
            w_out, w_out_layer = w_out_ab, e
            pp = matmul(hp, w_in_ab, e)
            ps = matmul(hs, w_in_ab, e)
            oa, ob, akt, av, bk, bv = ctx_ab_attention(pp, seq_p, n_heads_ab, diff_lambda[e],
                                                       diff_subln_g[e], lam_init)
            mix_p = [oa, ob]
            new_a_k.append(akt.reshape(bp, 2 * n_heads_ab, DQK_A, seq_p).transpose(0, 3, 1, 2))
            new_a_v.append(av.reshape(bp, seq_p, n_heads_ab, D_HEAD))
            new_b_k.append(bk.reshape(bp, seq_p, n_heads_ab, D_HEAD))
            new_b_v.append(bv.reshape(bp, seq_p, n_heads_ab, D_HEAD))
            bias = na_bias_table(nat_rpb[e], seq_s // GRID_W)
            mix_s = lat_ab_attention(
                ps, seq_s, n_heads_ab,
                cache_a_k[:, e].reshape(bs, past, w_ab), cache_a_v[:, e].reshape(bs, past, w_ab),
                cache_b_k[:, e].reshape(bs, past, w_ab), cache_b_v[:, e].reshape(bs, past, w_ab),
                bias, rope_tables_a(seq_s), diff_lambda[e], diff_subln_g[e], lam_init)
        else:
            o = l // 2
            w_out, w_out_layer = w_out_c, o
            lg = jnp.stack([ret_log_decay_fwd[o], ret_log_decay_bwd[o]])
            pp = matmul(hp, w_in_c, o, out_dtype=BF16)
            ps = matmul(hs, w_in_c, o, out_dtype=BF16)
            mix_p, s_f, s_b = ctx_retention(pp, seq_p, n_heads_c, lg, ret_norm_g[o])
            new_c_f.append(s_f)
            new_c_b.append(s_b)
            mix_s = lat_retention(ps, seq_s, n_heads_c, lg, ret_norm_g[o],
                                  state_c_fwd[:, o:o + 1], state_c_bwd[:, o:o + 1], rope_tables_c(seq_s))

        xp = matmul(mix_p, w_out, w_out_layer, mode="resgate", res=xp, gate=g1_p, rows_per_group=rows_p)
        xs = matmul(mix_s, w_out, w_out_layer, mode="resgate", res=xs, gate=g1_s, rows_per_group=seq_s)

        hp = norm_modulate(xp, norm_mlp_g[l], sh2_p, sc2_p, rows_p)
        hs = norm_modulate(xs, norm_mlp_g[l], sh2_s, sc2_s, seq_s)
        up = matmul(hp, w_mlp_in, l, mode="relu2", out_dtype=BF16)
        us = matmul(hs, w_mlp_in, l, mode="relu2", out_dtype=BF16)
        xp = matmul(up, w_mlp_out, l, mode="resgate", res=xp, gate=g2_p, rows_per_group=rows_p, tn=1024, tk=1024)
        xs = matmul(us, w_mlp_out, l, mode="resgate", res=xs, gate=g2_s, rows_per_group=seq_s, tn=1024, tk=1024)

    y_prompt = rms_norm(xp, final_norm_g).reshape(bp, seq_p, d)
    y_sample = rms_norm(xs, final_norm_g).reshape(bs, seq_s, d)
    return (y_prompt, y_sample,
            jnp.stack(new_a_k, axis=1), jnp.stack(new_a_v, axis=1),
            jnp.stack(new_b_k, axis=1), jnp.stack(new_b_v, axis=1),
            jnp.concatenate(new_c_f, axis=1), jnp.concatenate(new_c_b, axis=1))
```

```python
import functools

import jax
import jax.numpy as jnp
import numpy as np
from jax import lax
from jax.experimental import pallas as pl
from jax.experimental.pallas import tpu as pltpu

F32 = jnp.float32
BF16 = jnp.bfloat16

EPS = 1e-6
ROPE_BASE = 10000.0
GRID_W = 64
WIN_R = 8
WIN_C = 16
DQK_A = 64
D_HEAD = 128
DK_C = 256
LANES = 128
MOD_ROWS = 8
VMEM_LIMIT = 56 * 1024 * 1024

NT_DIMS = (((1,), (1,)), ((), ()))


def _params(*sem):
    return pltpu.CompilerParams(dimension_semantics=sem, vmem_limit_bytes=VMEM_LIMIT)


def _silu(x):
    return x / (1.0 + jnp.exp(-x))


def _modvec_kernel(c_ref, w_ref, b_ref, o_ref):
    s = _silu(c_ref[...]).astype(BF16)
    w = w_ref[0].astype(BF16)
    o_ref[0] = jnp.dot(s, w, preferred_element_type=F32) + b_ref[0]


def mod_vectors(cvec, w_mod, b_mod, tn=512):
    depth, d, n = w_mod.shape
    return pl.pallas_call(
        _modvec_kernel,
        grid=(depth, n // tn),
        in_specs=[
            pl.BlockSpec((MOD_ROWS, d), lambda l, j: (0, 0)),
            pl.BlockSpec((1, d, tn), lambda l, j: (l, 0, j)),
            pl.BlockSpec((1, 1, tn), lambda l, j: (l, 0, j)),
        ],
        out_specs=pl.BlockSpec((1, MOD_ROWS, tn), lambda l, j: (l, 0, j)),
        out_shape=jax.ShapeDtypeStruct((depth, MOD_ROWS, n), F32),
        compiler_params=_params("parallel", "parallel"),
        name="mod_vectors",
    )(cvec, w_mod, b_mod.reshape(depth, 1, n))


def _norm_mod_kernel(x_ref, g_ref, sh_ref, sc_ref, o_ref):
    x = x_ref[...]
    y = x * lax.rsqrt(jnp.mean(x * x, axis=-1, keepdims=True) + EPS) * g_ref[...]
    o_ref[...] = (y * (1.0 + sc_ref[0]) + sh_ref[0]).astype(o_ref.dtype)


def norm_modulate(x, g, shift, scale, rows_per_group, tr=512):
    m, d = x.shape
    grp = lambda i: ((i * tr) // rows_per_group, 0, 0)
    return pl.pallas_call(
        _norm_mod_kernel,
        grid=(m // tr,),
        in_specs=[
            pl.BlockSpec((tr, d), lambda i: (i, 0)),
            pl.BlockSpec((1, d), lambda i: (0, 0)),
            pl.BlockSpec((1, 1, d), grp),
            pl.BlockSpec((1, 1, d), grp),
        ],
        out_specs=pl.BlockSpec((tr, d), lambda i: (i, 0)),
        out_shape=jax.ShapeDtypeStruct((m, d), BF16),
        compiler_params=_params("parallel"),
        name="norm_modulate",
    )(x, g.reshape(1, d), shift, scale)


def _norm_kernel(x_ref, g_ref, o_ref):
    x = x_ref[...]
    o_ref[...] = x * lax.rsqrt(jnp.mean(x * x, axis=-1, keepdims=True) + EPS) * g_ref[...]


def rms_norm(x, g, tr=256):
    m, d = x.shape
    return pl.pallas_call(
        _norm_kernel,
        grid=(m // tr,),
        in_specs=[pl.BlockSpec((tr, d), lambda i: (i, 0)), pl.BlockSpec((1, d), lambda i: (0, 0))],
        out_specs=pl.BlockSpec((tr, d), lambda i: (i, 0)),
        out_shape=jax.ShapeDtypeStruct((m, d), F32),
        compiler_params=_params("parallel"),
        name="rms_norm",
    )(x, g.reshape(1, d))


def _mm_kernel(*refs, mode, nk, n_pairs, group_rows):
    x_refs = refs[:n_pairs]
    w_refs = refs[n_pairs:2 * n_pairs]
    refs = refs[2 * n_pairs:]
    if mode == "resgate":
        res_ref, gate_ref, o_ref = refs
    else:
        (o_ref,) = refs

    def product():
        acc = None
        for x_ref, w_ref in zip(x_refs, w_refs):
            t = jnp.dot(x_ref[...], w_ref[...].astype(BF16), preferred_element_type=F32)
            acc = t if acc is None else acc + t
        return acc

    def epilogue(acc):
        if mode == "relu2":
            r = jnp.maximum(acc, 0.0)
            o_ref[...] = (r * r).astype(o_ref.dtype)
        elif mode == "resgate":
            for g in range(gate_ref.shape[0]):
                rows = slice(g * group_rows, (g + 1) * group_rows)
                o_ref[rows, :] = res_ref[rows, :] + gate_ref[g] * acc[rows, :]
        else:
            o_ref[...] = acc.astype(o_ref.dtype)

    if nk == 1:
        epilogue(product())
        return

    k = pl.program_id(2)

    @pl.when(k == 0)
    def _():
        o_ref[...] = product()

    @pl.when(jnp.logical_and(k > 0, k < nk - 1))
    def _():
        o_ref[...] += product()

    @pl.when(k == nk - 1)
    def _():
        epilogue(o_ref[...] + product())


def matmul(xs, w, layer, *, mode="plain", out_dtype=F32, res=None, gate=None, rows_per_group=None,
           tm=2048, tn=512, tk=4096):
    xs = list(xs) if isinstance(xs, (list, tuple)) else [xs]
    n_pairs = len(xs)
    m, kpiece = xs[0].shape
    n = w.shape[2]
    tm = min(tm, m)
    tk = min(tk, kpiece)
    nk = kpiece // tk
    assert n_pairs == 1 or nk == 1
    assert nk == 1 or out_dtype == F32
    x_mode = dict(pipeline_mode=pl.Buffered(1)) if nk == 1 else {}
    in_specs = [pl.BlockSpec((tm, tk), lambda i, j, k: (i, k), **x_mode) for _ in xs]
    in_specs += [pl.BlockSpec((None, tk, tn), lambda i, j, k, piece=piece: (layer, k + piece, j))
                 for piece in range(n_pairs)]
    args = xs + [w] * n_pairs
    group_rows = None
    if mode == "resgate":
        group_rows = min(rows_per_group, tm)
        groups_per_tile = tm // group_rows
        tiles_per_group = rows_per_group // group_rows
        in_specs += [
            pl.BlockSpec((tm, tn), lambda i, j, k: (i, j)),
            pl.BlockSpec((groups_per_tile, 1, tn), lambda i, j, k: (i // tiles_per_group, 0, j)),
        ]
        args += [res, gate]
    return pl.pallas_call(
        functools.partial(_mm_kernel, mode=mode, nk=nk, n_pairs=n_pairs, group_rows=group_rows),
        grid=(m // tm, n // tn, nk),
        in_specs=in_specs,
        out_specs=pl.BlockSpec((tm, tn), lambda i, j, k: (i, j)),
        out_shape=jax.ShapeDtypeStruct((m, n), out_dtype),
        compiler_params=_params("parallel", "parallel", "arbitrary"),
        name="matmul_" + mode,
    )(*args)


def _softmax(s):
    e = jnp.exp(s - jnp.max(s, axis=-1, keepdims=True))
    return e * (1.0 / jnp.sum(e, axis=-1, keepdims=True))


def _exp_parts(scores):
    m = functools.reduce(jnp.maximum, [jnp.max(s, axis=-1, keepdims=True) for s in scores])
    es = [jnp.exp(s - m) for s in scores]
    den = functools.reduce(jnp.add, [jnp.sum(e, axis=-1, keepdims=True) for e in es])
    return [e.astype(BF16) for e in es], 1.0 / den


def _qk(q_bf, k_bf, keys_transposed=False):
    if keys_transposed:
        return jnp.dot(q_bf, k_bf, preferred_element_type=F32)
    return lax.dot_general(q_bf, k_bf, NT_DIMS, preferred_element_type=F32)


def _weighted_values(es, vs):
    o = None
    for e, v in zip(es, vs):
        t = jnp.dot(e, v, preferred_element_type=F32)
        o = t if o is None else o + t
    return o


def _diff_lambda(lp_ref, lam_init):
    lp = lp_ref[...]
    a = jnp.sum(lp[0:1] * lp[1:2], axis=-1, keepdims=True)
    b = jnp.sum(lp[2:3] * lp[3:4], axis=-1, keepdims=True)
    return jnp.exp(a) - jnp.exp(b) + lam_init


def _ab_scores(qa, kas, qb, kbs, biases, keys_transposed=False):
    lane = lax.broadcasted_iota(jnp.int32, qa.shape, 1)
    qs = qa * (DQK_A ** -0.5)
    q1 = jnp.where(lane < DQK_A, qs, 0.0).astype(BF16)
    q2 = jnp.where(lane >= DQK_A, qs, 0.0).astype(BF16)
    q3 = qb.astype(BF16)
    s3 = []
    for k, bias in zip(kbs, biases):
        s = _qk(q3, k) * (D_HEAD ** -0.5)
        s3.append(s if bias is None else s + bias)
    return ([_qk(q1, k, keys_transposed) for k in kas], [_qk(q2, k, keys_transposed) for k in kas], s3)


def _ab_values(parts, vas, vbs, lam, subln_g, lam_init):
    (e1, inv1), (e2, inv2), (e3, inv3) = parts
    o = _weighted_values(e1, vas) * inv1 - _weighted_values(e2, vas) * (lam * inv2)
    y = o * lax.rsqrt(jnp.mean(o * o, axis=-1, keepdims=True) + EPS) * subln_g
    return y * (1.0 - lam_init), _weighted_values(e3, vbs) * inv3


def _rope_a(x, cos, sin_lo, sin_hi):
    half = DQK_A // 4
    return x * cos + pltpu.roll(x, LANES - half, 1) * sin_lo + pltpu.roll(x, half, 1) * sin_hi


def _ctx_ab_kernel(qa_ref, ka_ref, va_ref, qb_ref, kb_ref, vb_ref, lp_ref, g_ref,
                   oa_ref, ob_ref, akt_ref, av_ref, bk_ref, bv_ref, *, lam_init, heads):
    lam = _diff_lambda(lp_ref, lam_init)
    seq = qa_ref.shape[0]
    akt_ref[0] = ka_ref[...].T
    sls = [slice(h * D_HEAD, (h + 1) * D_HEAD) for h in range(heads)]

    lane = lax.broadcasted_iota(jnp.int32, (seq, D_HEAD), 1)
    scale_a = DQK_A ** -0.5
    scale_b = D_HEAD ** -0.5

    def stage_scores(sl):
        k = akt_ref[0, sl, :].astype(BF16)
        q = qa_ref[:, sl]
        return (_qk(jnp.where(lane < DQK_A, q, 0.0).astype(BF16), k, True) * scale_a,
                _qk(jnp.where(lane >= DQK_A, q, 0.0).astype(BF16), k, True) * scale_a,
                _qk(qb_ref[:, sl].astype(BF16), kb_ref[:, sl].astype(BF16)) * scale_b)

    def stage_softmax(s):
        p1, p2, p3 = (_softmax(x) for x in s)
        return (p1 - lam * p2).astype(BF16), p3.astype(BF16)

    def stage_values(sl, p):
        o = jnp.dot(p[0], va_ref[:, sl].astype(BF16), preferred_element_type=F32)
        ob = jnp.dot(p[1], vb_ref[:, sl].astype(BF16), preferred_element_type=F32)
        y = o * lax.rsqrt(jnp.mean(o * o, axis=-1, keepdims=True) + EPS) * g_ref[...]
        oa_ref[:, sl] = (y * (1.0 - lam_init)).astype(oa_ref.dtype)
        ob_ref[:, sl] = ob.astype(ob_ref.dtype)

    scores, probs = {}, {}
    for t in range(heads + 2):
        if t < heads:
            scores[t] = stage_scores(sls[t])
        if 0 <= t - 1 < heads:
            probs[t - 1] = stage_softmax(scores.pop(t - 1))
        if 0 <= t - 2 < heads:
            stage_values(sls[t - 2], probs.pop(t - 2))
    av_ref[...] = va_ref[...].reshape(seq, heads, D_HEAD)
    bk_ref[...] = kb_ref[...].reshape(seq, heads, D_HEAD)
    bv_ref[...] = vb_ref[...].reshape(seq, heads, D_HEAD)


def ctx_ab_attention(p, seq, n_heads, diff_lambda, subln_g, lam_init, heads=8):
    m = p.shape[0]
    nb = m // seq
    nhb = n_heads // heads
    w = heads * D_HEAD
    blk = lambda part: pl.BlockSpec((seq, w), lambda b, h, part=part: (b, part * nhb + h))
    out2d = pl.BlockSpec((seq, w), lambda b, h: (b, h))
    cache = pl.BlockSpec((seq, heads, D_HEAD), lambda b, h: (b, h, 0))
    return pl.pallas_call(
        functools.partial(_ctx_ab_kernel, lam_init=lam_init, heads=heads),
        grid=(nb, nhb),
        in_specs=[blk(0), blk(1), blk(2), blk(3), blk(4), blk(5),
                  pl.BlockSpec((4, DQK_A), lambda b, h: (0, 0)),
                  pl.BlockSpec((1, D_HEAD), lambda b, h: (0, 0))],
        out_specs=[out2d, out2d, pl.BlockSpec((1, w, seq), lambda b, h: (b, h, 0)), cache, cache, cache],
        out_shape=[jax.ShapeDtypeStruct((m, n_heads * D_HEAD), BF16),
                   jax.ShapeDtypeStruct((m, n_heads * D_HEAD), BF16),
                   jax.ShapeDtypeStruct((nb, n_heads * D_HEAD, seq), F32),
                   jax.ShapeDtypeStruct((m, n_heads, D_HEAD), F32),
                   jax.ShapeDtypeStruct((m, n_heads, D_HEAD), F32),
                   jax.ShapeDtypeStruct((m, n_heads, D_HEAD), F32)],
        compiler_params=_params("parallel", "parallel"),
        name="ctx_ab_attention",
    )(p, p, p, p, p, p, diff_lambda, subln_g.reshape(1, D_HEAD))


def _na_bias_kernel(rpb_ref, o_ref, *, n_rows):
    h = pl.program_id(0)
    nd = 2 * WIN_C - 1
    ne = 2 * WIN_R - 1
    shape = (GRID_W, LANES)
    lane = lax.broadcasted_iota(jnp.int32, shape, 1)
    qc = lax.broadcasted_iota(jnp.int32, shape, 0)
    kc = lane & (GRID_W - 1)
    diff = kc - qc + (WIN_C - 1)
    cs = jnp.clip(qc - WIN_C // 2, 0, GRID_W - WIN_C)
    col_ok = (kc >= cs) & (kc < cs + WIN_C)
    low_half = lane < GRID_W
    neg = jnp.full(shape, -jnp.inf, F32)

    toeplitz = []
    for e in range(ne):
        acc = jnp.zeros(shape, F32)
        for d in range(nd):
            acc = jnp.where(diff == d, rpb_ref[h, e * nd + d], acc)
        toeplitz.append(jnp.where(col_ok, acc, neg))

    wr = min(WIN_R, n_rows)
    for r in range(n_rows):
        rs = min(max(r - wr // 2, 0), n_rows - wr)
        for pair in range(n_rows // 2):
            kr0, kr1 = 2 * pair, 2 * pair + 1
            a = toeplitz[kr0 - r + WIN_R - 1] if rs <= kr0 < rs + wr else neg
            b = toeplitz[kr1 - r + WIN_R - 1] if rs <= kr1 < rs + wr else neg
            blk = a if a is b else jnp.where(low_half, a, b)
            o_ref[0, r * GRID_W:(r + 1) * GRID_W, pair * LANES:(pair + 1) * LANES] = blk


def na_bias_table(rpb, n_rows):
    n_heads = rpb.shape[0]
    t = n_rows * GRID_W
    return pl.pallas_call(
        functools.partial(_na_bias_kernel, n_rows=n_rows),
        grid=(n_heads,),
        in_specs=[pl.BlockSpec(memory_space=pltpu.SMEM)],
        out_specs=pl.BlockSpec((1, t, t), lambda h: (h, 0, 0)),
        out_shape=jax.ShapeDtypeStruct((n_heads, t, t), F32),
        compiler_params=_params("parallel"),
        name="na_bias_table",
    )(rpb.reshape(n_heads, -1))


def _lat_ab_kernel(qa_ref, ka_ref, va_ref, qb_ref, kb_ref, vb_ref,
                   cak_ref, cav_ref, cbk_ref, cbv_ref, bias_ref,
                   qcos_ref, qslo_ref, qshi_ref, kcos_ref, kslo_ref, kshi_ref,
                   lp_ref, g_ref, oa_ref, ob_ref, *, lam_init):
    lam = _diff_lambda(lp_ref, lam_init)
    q = _rope_a(qa_ref[...], qcos_ref[...], qslo_ref[...], qshi_ref[...])
    k = _rope_a(ka_ref[...], kcos_ref[...], kslo_ref[...], kshi_ref[...])
    scores = _ab_scores(q, [k.astype(BF16), cak_ref[0].astype(BF16)],
                        qb_ref[...], [kb_ref[...].astype(BF16), cbk_ref[0].astype(BF16)], [bias_ref[0], None])
    parts = [_exp_parts(s) for s in scores]
    oa, ob = _ab_values(parts, [va_ref[...].astype(BF16), cav_ref[0].astype(BF16)],
                        [vb_ref[...].astype(BF16), cbv_ref[0].astype(BF16)], lam, g_ref[...], lam_init)
    oa_ref[...] = oa.astype(oa_ref.dtype)
    ob_ref[...] = ob.astype(ob_ref.dtype)


def lat_ab_attention(p, seq, n_heads, cache_a_k, cache_a_v, cache_b_k, cache_b_v, bias, rope,
                     diff_lambda, subln_g, lam_init, tq=256):
    m = p.shape[0]
    nb = m // seq
    nq = seq // tq
    past = cache_a_k.shape[1]
    qblk = lambda part: pl.BlockSpec((tq, D_HEAD), lambda b, h, i, part=part: (b * nq + i, part * n_heads + h))
    kblk = lambda part: pl.BlockSpec((seq, D_HEAD), lambda b, h, i, part=part: (b, part * n_heads + h))
    cblk = pl.BlockSpec((1, past, D_HEAD), lambda b, h, i: (b, 0, h))
    qtab = pl.BlockSpec((tq, D_HEAD), lambda b, h, i: (i, 0))
    ktab = pl.BlockSpec((seq, D_HEAD), lambda b, h, i: (0, 0))
    cos, sin_lo, sin_hi = rope
    return pl.pallas_call(
        functools.partial(_lat_ab_kernel, lam_init=lam_init),
        grid=(nb, n_heads, nq),
        in_specs=[qblk(0), kblk(1), kblk(2), qblk(3), kblk(4), kblk(5),
                  cblk, cblk, cblk, cblk,
                  pl.BlockSpec((1, tq, seq), lambda b, h, i: (h, i, 0)),
                  qtab, qtab, qtab, ktab, ktab, ktab,
                  pl.BlockSpec((4, DQK_A), lambda b, h, i: (0, 0)),
                  pl.BlockSpec((1, D_HEAD), lambda b, h, i: (0, 0))],
        out_specs=[pl.BlockSpec((tq, D_HEAD), lambda b, h, i: (b * nq + i, h))] * 2,
        out_shape=[jax.ShapeDtypeStruct((m, n_heads * D_HEAD), BF16)] * 2,
        compiler_params=_params("parallel", "parallel", "parallel"),
        name="lat_ab_attention",
    )(p, p, p, p, p, p, cache_a_k, cache_a_v, cache_b_k, cache_b_v, bias,
      cos, sin_lo, sin_hi, cos, sin_lo, sin_hi, diff_lambda, subln_g.reshape(1, D_HEAD))


def _decay_geometry(row0, tq, tk):
    i = lax.broadcasted_iota(jnp.int32, (tq, tk), 0) + row0
    j = lax.broadcasted_iota(jnp.int32, (tq, tk), 1)
    rel = i - j
    return jnp.abs(rel).astype(F32), rel >= 0, jnp.where(rel == 0, 2.0, 1.0)


def _decay_mask(geometry, lg_f, lg_b):
    dist, is_fwd, diag = geometry
    return jnp.exp(jnp.where(is_fwd, lg_f, lg_b) * dist) * diag


def _gated_head_norm(o, g, norm_g):
    y = o * lax.rsqrt(jnp.mean(o * o, axis=-1, keepdims=True) + EPS) * norm_g
    return _silu(g.astype(F32)) * y


def _ctx_ret_kernel(lg_ref, q_ref, k_ref, v_ref, g_ref, ng_ref, o_ref, sf_ref, sb_ref, *, heads):
    t = q_ref.shape[0]
    pos = lax.broadcasted_iota(jnp.int32, (t, 1), 0).astype(F32)
    geometry = _decay_geometry(0, t, t)
    for hh in range(heads):
        h = pl.program_id(1) * heads + hh
        sl = slice(hh * DK_C, (hh + 1) * DK_C)
        lg_f = lg_ref[0, h]
        lg_b = lg_ref[1, h]
        k = k_ref[:, sl] * (DK_C ** -0.5)
        v = v_ref[:, sl]
        s = lax.dot_general(q_ref[:, sl], k, NT_DIMS, preferred_element_type=F32)
        att = (s * _decay_mask(geometry, lg_f, lg_b)).astype(BF16)
        o = jnp.dot(att, v, preferred_element_type=F32)
        o_ref[:, sl] = _gated_head_norm(o, g_ref[:, sl], ng_ref[...]).astype(o_ref.dtype)
        kf32 = k.astype(F32)
        kf = (kf32 * jnp.exp(lg_f * (t - 1.0 - pos))).T.astype(BF16)
        kb = (kf32 * jnp.exp(lg_b * pos)).T.astype(BF16)
        sf_ref[hh] = jnp.dot(kf, v, preferred_element_type=F32)
        sb_ref[hh] = jnp.dot(kb, v, preferred_element_type=F32)


def ctx_retention(p, seq, n_heads, lg, norm_g, heads=4):
    m = p.shape[0]
    nb = m // seq
    nhb = n_heads // heads
    w = heads * DK_C
    blk = lambda part: pl.BlockSpec((seq, w), lambda b, h, part=part: (b, part * nhb + h))
    state = pl.BlockSpec((None, None, heads, DK_C, DK_C), lambda b, h: (b, 0, h, 0, 0))
    return pl.pallas_call(
        functools.partial(_ctx_ret_kernel, heads=heads),
        grid=(nb, nhb),
        in_specs=[pl.BlockSpec(memory_space=pltpu.SMEM), blk(0), blk(1), blk(2), blk(3),
                  pl.BlockSpec((1, DK_C), lambda b, h: (0, 0))],
        out_specs=[pl.BlockSpec((seq, w), lambda b, h: (b, h)), state, state],
        out_shape=[jax.ShapeDtypeStruct((m, n_heads * DK_C), BF16),
                   jax.ShapeDtypeStruct((nb, 1, n_heads, DK_C, DK_C), F32),
                   jax.ShapeDtypeStruct((nb, 1, n_heads, DK_C, DK_C), F32)],
        compiler_params=_params("parallel", "parallel"),
        name="ctx_retention",
    )(lg, p, p, p, p, norm_g.reshape(1, DK_C))


def _rope_c(x, cos, sin):
    halves = []
    for c in range(DK_C // LANES):
        sl = slice(c * LANES, (c + 1) * LANES)
        xs = x[:, sl]
        halves.append(xs * cos[:, sl] + pltpu.roll(xs, LANES // 2, 1) * sin[:, sl])
    return jnp.concatenate(halves, axis=-1)


def _lat_ret_kernel(lg_ref, q_ref, k_ref, v_ref, g_ref, s0f_ref, s0b_ref, ng_ref,
                    qcos_ref, qsin_ref, kcos_ref, ksin_ref, o_ref, *, tq):
    h = pl.program_id(1)
    row0 = pl.program_id(2) * tq
    lg_f = lg_ref[0, h]
    lg_b = lg_ref[1, h]
    t = k_ref.shape[0]
    q = _rope_c(q_ref[...].astype(F32), qcos_ref[...], qsin_ref[...]).astype(BF16)
    k = _rope_c(k_ref[...].astype(F32) * (DK_C ** -0.5), kcos_ref[...], ksin_ref[...]).astype(BF16)
    s = lax.dot_general(q, k, NT_DIMS, preferred_element_type=F32)
    att = (s * _decay_mask(_decay_geometry(row0, tq, t), lg_f, lg_b)).astype(BF16)
    o = jnp.dot(att, v_ref[...], preferred_element_type=F32)
    pos = (lax.broadcasted_iota(jnp.int32, (tq, 1), 0) + row0).astype(F32)
    o += jnp.dot(q, s0f_ref[...].astype(BF16), preferred_element_type=F32) * jnp.exp(lg_f * (pos + 1.0))
    o += jnp.dot(q, s0b_ref[...].astype(BF16), preferred_element_type=F32) * jnp.exp(lg_b * (t - pos))
    o_ref[...] = _gated_head_norm(o, g_ref[...], ng_ref[...]).astype(o_ref.dtype)


def lat_retention(p, seq, n_heads, lg, norm_g, s0_f, s0_b, rope, tq=1024):
    m = p.shape[0]
    nb = m // seq
    nq = seq // tq
    qblk = lambda part: pl.BlockSpec((tq, DK_C), lambda b, h, i, part=part: (b * nq + i, part * n_heads + h))
    kblk = lambda part: pl.BlockSpec((seq, DK_C), lambda b, h, i, part=part: (b, part * n_heads + h))
    state = pl.BlockSpec((None, None, None, DK_C, DK_C), lambda b, h, i: (b, 0, h, 0, 0))
    qtab = pl.BlockSpec((tq, DK_C), lambda b, h, i: (i, 0))
    ktab = pl.BlockSpec((seq, DK_C), lambda b, h, i: (0, 0))
    cos, sin = rope
    return pl.pallas_call(
        functools.partial(_lat_ret_kernel, tq=tq),
        grid=(nb, n_heads, nq),
        in_specs=[pl.BlockSpec(memory_space=pltpu.SMEM), qblk(0), kblk(1), kblk(2), qblk(3),
                  state, state, pl.BlockSpec((1, DK_C), lambda b, h, i: (0, 0)),
                  qtab, qtab, ktab, ktab],
        out_specs=pl.BlockSpec((tq, DK_C), lambda b, h, i: (b * nq + i, h)),
        out_shape=jax.ShapeDtypeStruct((m, n_heads * DK_C), BF16),
        compiler_params=_params("parallel", "parallel", "parallel"),
        name="lat_retention",
    )(lg, p, p, p, p, s0_f, s0_b, norm_g.reshape(1, DK_C), cos, sin, cos, sin)


def _rope_tables(seq, width, group):
    half = group // 2
    t = np.arange(seq)
    rows, cols = t // GRID_W, t % GRID_W
    lane = np.arange(width)
    inv = (ROPE_BASE ** (-jnp.arange(half, dtype=F32) / half))[lane % half]
    use_cols = (lane // group) % 2 == 1
    pos = np.where(use_cols[None, :], cols[:, None], rows[:, None])
    ang = jnp.asarray(pos, dtype=jnp.int32).astype(F32) * inv[None, :]
    first = jnp.asarray((lane % group) < half)
    return jnp.cos(ang), jnp.sin(ang), first


def rope_tables_a(seq):
    cos, sin, first = _rope_tables(seq, D_HEAD, DQK_A // 2)
    return cos, jnp.where(first, -sin, 0.0), jnp.where(first, 0.0, sin)


def rope_tables_c(seq):
    cos, sin, first = _rope_tables(seq, DK_C, DK_C // 2)
    return cos, jnp.where(first, -sin, sin)


def kernel(x_prompt, x_sample, c, cache_a_k, cache_a_v, cache_b_k, cache_b_v, state_c_fwd, state_c_bwd,
           c_ctx, norm_mix_g, norm_mlp_g, w_mod, b_mod, w_mlp_in, w_mlp_out, w_in_ab, w_out_ab,
           diff_lambda, diff_subln_g, nat_rpb, w_in_c, w_out_c, ret_log_decay_fwd, ret_log_decay_bwd,
           ret_norm_g, final_norm_g):
    bp, seq_p, d = x_prompt.shape
    bs, seq_s, _ = x_sample.shape
    depth = w_mod.shape[0]
    n_heads_ab = w_out_ab.shape[1] // (2 * D_HEAD)
    n_heads_c = w_out_c.shape[1] // DK_C
    past = cache_a_k.shape[2]
    w_ab = n_heads_ab * D_HEAD

    xp = x_prompt.reshape(bp * seq_p, d)
    xs = x_sample.reshape(bs * seq_s, d)
    rows_p = bp * seq_p

    cvec = jnp.concatenate([c_ctx[None, :], c, jnp.zeros((MOD_ROWS - 1 - bs, d), F32)], axis=0)
    mod = mod_vectors(cvec, w_mod, b_mod).reshape(depth, MOD_ROWS, 6, 1, d)

    new_a_k, new_a_v, new_b_k, new_b_v, new_c_f, new_c_b = [], [], [], [], [], []

    for l in range(depth):
        sh1_p, sc1_p, g1_p, sh2_p, sc2_p, g2_p = (mod[l, 0:1, i] for i in range(6))
        sh1_s, sc1_s, g1_s, sh2_s, sc2_s, g2_s = (mod[l, 1:1 + bs, i] for i in range(6))
        hp = norm_modulate(xp, norm_mix_g[l], sh1_p, sc1_p, rows_p)
        hs = norm_modulate(xs, norm_mix_g[l], sh1_s, sc1_s, seq_s)

        if l % 2 == 0:
            e = l // 2
            lam_init = 0.8 - 0.6 * float(np.exp(-0.3 * l))
            w_out, w_out_layer = w_out_ab, e
            pp = matmul(hp, w_in_ab, e)
            ps = matmul(hs, w_in_ab, e)
            oa, ob, akt, av, bk, bv = ctx_ab_attention(pp, seq_p, n_heads_ab, diff_lambda[e],
                                                       diff_subln_g[e], lam_init)
            mix_p = [oa, ob]
            new_a_k.append(akt.reshape(bp, 2 * n_heads_ab, DQK_A, seq_p).transpose(0, 3, 1, 2))
            new_a_v.append(av.reshape(bp, seq_p, n_heads_ab, D_HEAD))
            new_b_k.append(bk.reshape(bp, seq_p, n_heads_ab, D_HEAD))
            new_b_v.append(bv.reshape(bp, seq_p, n_heads_ab, D_HEAD))
            bias = na_bias_table(nat_rpb[e], seq_s // GRID_W)
            mix_s = lat_ab_attention(
                ps, seq_s, n_heads_ab,
                cache_a_k[:, e].reshape(bs, past, w_ab), cache_a_v[:, e].reshape(bs, past, w_ab),
                cache_b_k[:, e].reshape(bs, past, w_ab), cache_b_v[:, e].reshape(bs, past, w_ab),
                bias, rope_tables_a(seq_s), diff_lambda[e], diff_subln_g[e], lam_init)
        else:
            o = l // 2
            w_out, w_out_layer = w_out_c, o
            lg = jnp.stack([ret_log_decay_fwd[o], ret_log_decay_bwd[o]])
            pp = matmul(hp, w_in_c, o, out_dtype=BF16)
            ps = matmul(hs, w_in_c, o, out_dtype=BF16)
            mix_p, s_f, s_b = ctx_retention(pp, seq_p, n_heads_c, lg, ret_norm_g[o])
            new_c_f.append(s_f)
            new_c_b.append(s_b)
            mix_s = lat_retention(ps, seq_s, n_heads_c, lg, ret_norm_g[o],
                                  state_c_fwd[:, o:o + 1], state_c_bwd[:, o:o + 1], rope_tables_c(seq_s))

        xp = matmul(mix_p, w_out, w_out_layer, mode="resgate", res=xp, gate=g1_p, rows_per_group=rows_p)
        xs = matmul(mix_s, w_out, w_out_layer, mode="resgate", res=xs, gate=g1_s, rows_per_group=seq_s)

        hp = norm_modulate(xp, norm_mlp_g[l], sh2_p, sc2_p, rows_p)
        hs = norm_modulate(xs, norm_mlp_g[l], sh2_s, sc2_s, seq_s)
        up = matmul(hp, w_mlp_in, l, mode="relu2", out_dtype=BF16)
        us = matmul(hs, w_mlp_in, l, mode="relu2", out_dtype=BF16)
        xp = matmul(up, w_mlp_out, l, mode="resgate", res=xp, gate=g2_p, rows_per_group=rows_p, tn=1024, tk=1024)
        xs = matmul(us, w_mlp_out, l, mode="resgate", res=xs, gate=g2_s, rows_per_group=seq_s, tn=1024, tk=1024)

    y_prompt = rms_norm(xp, final_norm_g).reshape(bp, seq_p, d)
    y_sample = rms_norm(xs, final_norm_g).reshape(bs, seq_s, d)
    return (y_prompt, y_sample,
            jnp.stack(new_a_k, axis=1), jnp.stack(new_a_v, axis=1),
            jnp.stack(new_b_k, axis=1), jnp.stack(new_b_v, axis=1),
            jnp.concatenate(new_c_f, axis=1), jnp.concatenate(new_c_b, axis=1))
```

```python
import functools

import jax
import jax.numpy as jnp
import numpy as np
from jax import lax
from jax.experimental import pallas as pl
from jax.experimental.pallas import tpu as pltpu

F32 = jnp.float32
BF16 = jnp.bfloat16

EPS = 1e-6
ROPE_BASE = 10000.0
GRID_W = 64
WIN_R = 8
WIN_C = 16
DQK_A = 64
D_HEAD = 128
DK_C = 256
LANES = 128
MOD_ROWS = 8
VMEM_LIMIT = 56 * 1024 * 1024

NT_DIMS = (((1,), (1,)), ((), ()))


def _params(*sem):
    return pltpu.CompilerParams(dimension_semantics=sem, vmem_limit_bytes=VMEM_LIMIT)


def _silu(x):
    return x / (1.0 + jnp.exp(-x))


def _modvec_kernel(c_ref, w_ref, b_ref, o_ref):
    s = _silu(c_ref[...]).astype(BF16)
    w = w_ref[0].astype(BF16)
    o_ref[0] = jnp.dot(s, w, preferred_element_type=F32) + b_ref[0]


def mod_vectors(cvec, w_mod, b_mod, tn=512):
    depth, d, n = w_mod.shape
    return pl.pallas_call(
        _modvec_kernel,
        grid=(depth, n // tn),
        in_specs=[
            pl.BlockSpec((MOD_ROWS, d), lambda l, j: (0, 0)),
            pl.BlockSpec((1, d, tn), lambda l, j: (l, 0, j)),
            pl.BlockSpec((1, 1, tn), lambda l, j: (l, 0, j)),
        ],
        out_specs=pl.BlockSpec((1, MOD_ROWS, tn), lambda l, j: (l, 0, j)),
        out_shape=jax.ShapeDtypeStruct((depth, MOD_ROWS, n), F32),
        compiler_params=_params("parallel", "parallel"),
        name="mod_vectors",
    )(cvec, w_mod, b_mod.reshape(depth, 1, n))


def _norm_mod_kernel(x_ref, g_ref, sh_ref, sc_ref, o_ref):
    x = x_ref[...]
    y = x * lax.rsqrt(jnp.mean(x * x, axis=-1, keepdims=True) + EPS) * g_ref[...]
    o_ref[...] = (y * (1.0 + sc_ref[0]) + sh_ref[0]).astype(o_ref.dtype)


def norm_modulate(x, g, shift, scale, rows_per_group, tr=512):
    m, d = x.shape
    grp = lambda i: ((i * tr) // rows_per_group, 0, 0)
    return pl.pallas_call(
        _norm_mod_kernel,
        grid=(m // tr,),
        in_specs=[
            pl.BlockSpec((tr, d), lambda i: (i, 0)),
            pl.BlockSpec((1, d), lambda i: (0, 0)),
            pl.BlockSpec((1, 1, d), grp),
            pl.BlockSpec((1, 1, d), grp),
        ],
        out_specs=pl.BlockSpec((tr, d), lambda i: (i, 0)),
        out_shape=jax.ShapeDtypeStruct((m, d), BF16),
        compiler_params=_params("parallel"),
        name="norm_modulate",
    )(x, g.reshape(1, d), shift, scale)


def _norm_kernel(x_ref, g_ref, o_ref):
    x = x_ref[...]
    o_ref[...] = x * lax.rsqrt(jnp.mean(x * x, axis=-1, keepdims=True) + EPS) * g_ref[...]


def rms_norm(x, g, tr=256):
    m, d = x.shape
    return pl.pallas_call(
        _norm_kernel,
        grid=(m // tr,),
        in_specs=[pl.BlockSpec((tr, d), lambda i: (i, 0)), pl.BlockSpec((1, d), lambda i: (0, 0))],
        out_specs=pl.BlockSpec((tr, d), lambda i: (i, 0)),
        out_shape=jax.ShapeDtypeStruct((m, d), F32),
        compiler_params=_params("parallel"),
        name="rms_norm",
    )(x, g.reshape(1, d))


def _mm_kernel(*refs, mode, nk, n_pairs, group_rows):
    x_refs = refs[:n_pairs]
    w_refs = refs[n_pairs:2 * n_pairs]
    refs = refs[2 * n_pairs:]
    if mode == "resgate":
        res_ref, gate_ref, o_ref = refs
    else:
        (o_ref,) = refs

    def product():
        acc = None
        for x_ref, w_ref in zip(x_refs, w_refs):
            t = jnp.dot(x_ref[...], w_ref[...].astype(BF16), preferred_element_type=F32)
            acc = t if acc is None else acc + t
        return acc

    def epilogue(acc):
        if mode == "relu2":
            r = jnp.maximum(acc, 0.0)
            o_ref[...] = (r * r).astype(o_ref.dtype)
        elif mode == "resgate":
            for g in range(gate_ref.shape[0]):
                rows = slice(g * group_rows, (g + 1) * group_rows)
                o_ref[rows, :] = res_ref[rows, :] + gate_ref[g] * acc[rows, :]
        else:
            o_ref[...] = acc.astype(o_ref.dtype)

    if nk == 1:
        epilogue(product())
        return

    k = pl.program_id(2)

    @pl.when(k == 0)
    def _():
        o_ref[...] = product()

    @pl.when(jnp.logical_and(k > 0, k < nk - 1))
    def _():
        o_ref[...] += product()

    @pl.when(k == nk - 1)
    def _():
        epilogue(o_ref[...] + product())


def matmul(xs, w, layer, *, mode="plain", out_dtype=F32, res=None, gate=None, rows_per_group=None,
           tm=2048, tn=512, tk=4096):
    xs = list(xs) if isinstance(xs, (list, tuple)) else [xs]
    n_pairs = len(xs)
    m, kpiece = xs[0].shape
    n = w.shape[2]
    tm = min(tm, m)
    tk = min(tk, kpiece)
    nk = kpiece // tk
    assert n_pairs == 1 or nk == 1
    assert nk == 1 or out_dtype == F32
    x_mode = dict(pipeline_mode=pl.Buffered(1)) if nk == 1 else {}
    in_specs = [pl.BlockSpec((tm, tk), lambda i, j, k: (i, k), **x_mode) for _ in xs]
    in_specs += [pl.BlockSpec((None, tk, tn), lambda i, j, k, piece=piece: (layer, k + piece, j))
                 for piece in range(n_pairs)]
    args = xs + [w] * n_pairs
    group_rows = None
    if mode == "resgate":
        group_rows = min(rows_per_group, tm)
        groups_per_tile = tm // group_rows
        tiles_per_group = rows_per_group // group_rows
        in_specs += [
            pl.BlockSpec((tm, tn), lambda i, j, k: (i, j)),
            pl.BlockSpec((groups_per_tile, 1, tn), lambda i, j, k: (i // tiles_per_group, 0, j)),
        ]
        args += [res, gate]
    return pl.pallas_call(
        functools.partial(_mm_kernel, mode=mode, nk=nk, n_pairs=n_pairs, group_rows=group_rows),
        grid=(m // tm, n // tn, nk),
        in_specs=in_specs,
        out_specs=pl.BlockSpec((tm, tn), lambda i, j, k: (i, j)),
        out_shape=jax.ShapeDtypeStruct((m, n), out_dtype),
        compiler_params=_params("parallel", "parallel", "arbitrary"),
        name="matmul_" + mode,
    )(*args)


def _exp_parts(chains):
    ms = [functools.reduce(jnp.maximum, [jnp.max(s, axis=-1, keepdims=True) for s in c]) for c in chains]
    es = [[jnp.exp(s - m) for s in c] for c, m in zip(chains, ms)]
    dens = [functools.reduce(jnp.add, [jnp.sum(e, axis=-1, keepdims=True) for e in ec]) for ec in es]
    return [([e.astype(BF16) for e in ec], 1.0 / d) for ec, d in zip(es, dens)]


def _qk(q_bf, k_bf, keys_transposed=False):
    if keys_transposed:
        return jnp.dot(q_bf, k_bf, preferred_element_type=F32)
    return lax.dot_general(q_bf, k_bf, NT_DIMS, preferred_element_type=F32)


def _weighted_values(es, vs):
    o = None
    for e, v in zip(es, vs):
        t = jnp.dot(e, v, preferred_element_type=F32)
        o = t if o is None else o + t
    return o


def _diff_lambda(lp_ref, lam_init):
    lp = lp_ref[...]
    a = jnp.sum(lp[0:1] * lp[1:2], axis=-1, keepdims=True)
    b = jnp.sum(lp[2:3] * lp[3:4], axis=-1, keepdims=True)
    return jnp.exp(a) - jnp.exp(b) + lam_init


def _ab_scores(qa, kas, qb, kbs, biases, keys_transposed=False):
    lane = lax.broadcasted_iota(jnp.int32, qa.shape, 1)
    qs = qa * (DQK_A ** -0.5)
    q1 = jnp.where(lane < DQK_A, qs, 0.0).astype(BF16)
    q2 = jnp.where(lane >= DQK_A, qs, 0.0).astype(BF16)
    q3 = qb.astype(BF16)
    s3 = []
    for k, bias in zip(kbs, biases):
        s = _qk(q3, k) * (D_HEAD ** -0.5)
        s3.append(s if bias is None else s + bias)
    return ([_qk(q1, k, keys_transposed) for k in kas], [_qk(q2, k, keys_transposed) for k in kas], s3)


def _ab_values(parts, vas, vbs, lam, subln_g, lam_init):
    (e1, inv1), (e2, inv2), (e3, inv3) = parts
    o = _weighted_values(e1, vas) * inv1 - _weighted_values(e2, vas) * (lam * inv2)
    y = o * lax.rsqrt(jnp.mean(o * o, axis=-1, keepdims=True) + EPS) * subln_g
    return y * (1.0 - lam_init), _weighted_values(e3, vbs) * inv3


def _rope_a(x, cos, sin_lo, sin_hi):
    half = DQK_A // 4
    return x * cos + pltpu.roll(x, LANES - half, 1) * sin_lo + pltpu.roll(x, half, 1) * sin_hi


def _ctx_ab_kernel(qa_ref, ka_ref, va_ref, qb_ref, kb_ref, vb_ref, lp_ref, g_ref,
                   oa_ref, ob_ref, akt_ref, av_ref, bk_ref, bv_ref, *, lam_init, heads):
    lam = _diff_lambda(lp_ref, lam_init)
    seq = qa_ref.shape[0]
    akt_ref[0] = ka_ref[...].T
    sls = [slice(h * D_HEAD, (h + 1) * D_HEAD) for h in range(heads)]

    lane = lax.broadcasted_iota(jnp.int32, (seq, D_HEAD), 1)
    scale_a = DQK_A ** -0.5
    scale_b = D_HEAD ** -0.5

    def stage_scores(sl):
        k = akt_ref[0, sl, :].astype(BF16)
        q = qa_ref[:, sl]
        return (_qk(jnp.where(lane < DQK_A, q, 0.0).astype(BF16), k, True) * scale_a,
                _qk(jnp.where(lane >= DQK_A, q, 0.0).astype(BF16), k, True) * scale_a,
                _qk(qb_ref[:, sl].astype(BF16), kb_ref[:, sl].astype(BF16)) * scale_b)

    def stage_softmax(s):
        ms = [jnp.max(x, axis=-1, keepdims=True) for x in s]
        es = [jnp.exp(x - m) for x, m in zip(s, ms)]
        invs = [1.0 / jnp.sum(e, axis=-1, keepdims=True) for e in es]
        p1, p2, p3 = (e * i for e, i in zip(es, invs))
        return (p1 - lam * p2).astype(BF16), p3.astype(BF16)

    def stage_values(sl, p):
        o = jnp.dot(p[0], va_ref[:, sl].astype(BF16), preferred_element_type=F32)
        ob = jnp.dot(p[1], vb_ref[:, sl].astype(BF16), preferred_element_type=F32)
        y = o * lax.rsqrt(jnp.mean(o * o, axis=-1, keepdims=True) + EPS) * g_ref[...]
        oa_ref[:, sl] = (y * (1.0 - lam_init)).astype(oa_ref.dtype)
        ob_ref[:, sl] = ob.astype(ob_ref.dtype)

    scores, probs = {}, {}
    for t in range(heads + 2):
        if t < heads:
            scores[t] = stage_scores(sls[t])
        if 0 <= t - 1 < heads:
            probs[t - 1] = stage_softmax(scores.pop(t - 1))
        if 0 <= t - 2 < heads:
            stage_values(sls[t - 2], probs.pop(t - 2))
    av_ref[...] = va_ref[...].reshape(seq, heads, D_HEAD)
    bk_ref[...] = kb_ref[...].reshape(seq, heads, D_HEAD)
    bv_ref[...] = vb_ref[...].reshape(seq, heads, D_HEAD)


def ctx_ab_attention(p, seq, n_heads, diff_lambda, subln_g, lam_init, heads=8):
    m = p.shape[0]
    nb = m // seq
    nhb = n_heads // heads
    w = heads * D_HEAD
    blk = lambda part: pl.BlockSpec((seq, w), lambda b, h, part=part: (b, part * nhb + h))
    out2d = pl.BlockSpec((seq, w), lambda b, h: (b, h))
    cache = pl.BlockSpec((seq, heads, D_HEAD), lambda b, h: (b, h, 0))
    return pl.pallas_call(
        functools.partial(_ctx_ab_kernel, lam_init=lam_init, heads=heads),
        grid=(nb, nhb),
        in_specs=[blk(0), blk(1), blk(2), blk(3), blk(4), blk(5),
                  pl.BlockSpec((4, DQK_A), lambda b, h: (0, 0)),
                  pl.BlockSpec((1, D_HEAD), lambda b, h: (0, 0))],
        out_specs=[out2d, out2d, pl.BlockSpec((1, w, seq), lambda b, h: (b, h, 0)), cache, cache, cache],
        out_shape=[jax.ShapeDtypeStruct((m, n_heads * D_HEAD), BF16),
                   jax.ShapeDtypeStruct((m, n_heads * D_HEAD), BF16),
                   jax.ShapeDtypeStruct((nb, n_heads * D_HEAD, seq), F32),
                   jax.ShapeDtypeStruct((m, n_heads, D_HEAD), F32),
                   jax.ShapeDtypeStruct((m, n_heads, D_HEAD), F32),
                   jax.ShapeDtypeStruct((m, n_heads, D_HEAD), F32)],
        compiler_params=_params("parallel", "parallel"),
        name="ctx_ab_attention",
    )(p, p, p, p, p, p, diff_lambda, subln_g.reshape(1, D_HEAD))


def _na_bias_kernel(rpb_ref, o_ref, *, n_rows):
    h = pl.program_id(0)
    nd = 2 * WIN_C - 1
    ne = 2 * WIN_R - 1
    shape = (GRID_W, LANES)
    lane = lax.broadcasted_iota(jnp.int32, shape, 1)
    qc = lax.broadcasted_iota(jnp.int32, shape, 0)
    kc = lane & (GRID_W - 1)
    diff = kc - qc + (WIN_C - 1)
    cs = jnp.clip(qc - WIN_C // 2, 0, GRID_W - WIN_C)
    col_ok = (kc >= cs) & (kc < cs + WIN_C)
    low_half = lane < GRID_W
    neg = jnp.full(shape, -jnp.inf, F32)

    toeplitz = []
    for e in range(ne):
        acc = jnp.zeros(shape, F32)
        for d in range(nd):
            acc = jnp.where(diff == d, rpb_ref[h, e * nd + d], acc)
        toeplitz.append(jnp.where(col_ok, acc, neg))

    wr = min(WIN_R, n_rows)
    for r in range(n_rows):
        rs = min(max(r - wr // 2, 0), n_rows - wr)
        for pair in range(n_rows // 2):
            kr0, kr1 = 2 * pair, 2 * pair + 1
            a = toeplitz[kr0 - r + WIN_R - 1] if rs <= kr0 < rs + wr else neg
            b = toeplitz[kr1 - r + WIN_R - 1] if rs <= kr1 < rs + wr else neg
            blk = a if a is b else jnp.where(low_half, a, b)
            o_ref[0, r * GRID_W:(r + 1) * GRID_W, pair * LANES:(pair + 1) * LANES] = blk


def na_bias_table(rpb, n_rows):
    n_heads = rpb.shape[0]
    t = n_rows * GRID_W
    return pl.pallas_call(
        functools.partial(_na_bias_kernel, n_rows=n_rows),
        grid=(n_heads,),
        in_specs=[pl.BlockSpec(memory_space=pltpu.SMEM)],
        out_specs=pl.BlockSpec((1, t, t), lambda h: (h, 0, 0)),
        out_shape=jax.ShapeDtypeStruct((n_heads, t, t), F32),
        compiler_params=_params("parallel"),
        name="na_bias_table",
    )(rpb.reshape(n_heads, -1))


def _lat_ab_kernel(qa_ref, ka_ref, va_ref, qb_ref, kb_ref, vb_ref,
                   cak_ref, cav_ref, cbk_ref, cbv_ref, bias_ref,
                   qcos_ref, qslo_ref, qshi_ref, kcos_ref, kslo_ref, kshi_ref,
                   lp_ref, g_ref, oa_ref, ob_ref, *, lam_init):
    lam = _diff_lambda(lp_ref, lam_init)
    q = _rope_a(qa_ref[...], qcos_ref[...], qslo_ref[...], qshi_ref[...])
    k = _rope_a(ka_ref[...], kcos_ref[...], kslo_ref[...], kshi_ref[...])
    scores = _ab_scores(q, [k.astype(BF16), cak_ref[0].astype(BF16)],
                        qb_ref[...], [kb_ref[...].astype(BF16), cbk_ref[0].astype(BF16)], [bias_ref[0], None])
    parts = _exp_parts(scores)
    oa, ob = _ab_values(parts, [va_ref[...].astype(BF16), cav_ref[0].astype(BF16)],
                        [vb_ref[...].astype(BF16), cbv_ref[0].astype(BF16)], lam, g_ref[...], lam_init)
    oa_ref[...] = oa.astype(oa_ref.dtype)
    ob_ref[...] = ob.astype(ob_ref.dtype)


def lat_ab_attention(p, seq, n_heads, cache_a_k, cache_a_v, cache_b_k, cache_b_v, bias, rope,
                     diff_lambda, subln_g, lam_init, tq=1024):
    m = p.shape[0]
    nb = m // seq
    nq = seq // tq
    past = cache_a_k.shape[1]
    qblk = lambda part: pl.BlockSpec((tq, D_HEAD), lambda b, h, i, part=part: (b * nq + i, part * n_heads + h))
    kblk = lambda part: pl.BlockSpec((seq, D_HEAD), lambda b, h, i, part=part: (b, part * n_heads + h))
    cblk = pl.BlockSpec((1, past, D_HEAD), lambda b, h, i: (b, 0, h))
    qtab = pl.BlockSpec((tq, D_HEAD), lambda b, h, i: (i, 0))
    ktab = pl.BlockSpec((seq, D_HEAD), lambda b, h, i: (0, 0))
    cos, sin_lo, sin_hi = rope
    return pl.pallas_call(
        functools.partial(_lat_ab_kernel, lam_init=lam_init),
        grid=(nb, n_heads, nq),
        in_specs=[qblk(0), kblk(1), kblk(2), qblk(3), kblk(4), kblk(5),
                  cblk, cblk, cblk, cblk,
                  pl.BlockSpec((1, tq, seq), lambda b, h, i: (h, i, 0)),
                  qtab, qtab, qtab, ktab, ktab, ktab,
                  pl.BlockSpec((4, DQK_A), lambda b, h, i: (0, 0)),
                  pl.BlockSpec((1, D_HEAD), lambda b, h, i: (0, 0))],
        out_specs=[pl.BlockSpec((tq, D_HEAD), lambda b, h, i: (b * nq + i, h))] * 2,
        out_shape=[jax.ShapeDtypeStruct((m, n_heads * D_HEAD), BF16)] * 2,
        compiler_params=_params("parallel", "parallel", "parallel"),
        name="lat_ab_attention",
    )(p, p, p, p, p, p, cache_a_k, cache_a_v, cache_b_k, cache_b_v, bias,
      cos, sin_lo, sin_hi, cos, sin_lo, sin_hi, diff_lambda, subln_g.reshape(1, D_HEAD))


def _decay_geometry(row0, tq, tk):
    i = lax.broadcasted_iota(jnp.int32, (tq, tk), 0) + row0
    j = lax.broadcasted_iota(jnp.int32, (tq, tk), 1)
    rel = i - j
    return jnp.abs(rel).astype(F32), rel >= 0, jnp.where(rel == 0, 2.0, 1.0)


def _decay_mask(geometry, lg_f, lg_b):
    dist, is_fwd, diag = geometry
    return jnp.exp(jnp.where(is_fwd, lg_f, lg_b) * dist) * diag


def _gated_head_norm(o, g, norm_g):
    y = o * lax.rsqrt(jnp.mean(o * o, axis=-1, keepdims=True) + EPS) * norm_g
    return _silu(g.astype(F32)) * y


def _ctx_ret_kernel(lg_ref, q_ref, k_ref, v_ref, g_ref, ng_ref, o_ref, sf_ref, sb_ref, *, heads):
    t = q_ref.shape[0]
    pos = lax.broadcasted_iota(jnp.int32, (t, 1), 0).astype(F32)
    geometry = _decay_geometry(0, t, t)
    for hh in range(heads):
        h = pl.program_id(1) * heads + hh
        sl = slice(hh * DK_C, (hh + 1) * DK_C)
        lg_f = lg_ref[0, h]
        lg_b = lg_ref[1, h]
        k = k_ref[:, sl] * (DK_C ** -0.5)
        v = v_ref[:, sl]
        s = lax.dot_general(q_ref[:, sl], k, NT_DIMS, preferred_element_type=F32)
        att = (s * _decay_mask(geometry, lg_f, lg_b)).astype(BF16)
        o = jnp.dot(att, v, preferred_element_type=F32)
        o_ref[:, sl] = _gated_head_norm(o, g_ref[:, sl], ng_ref[...]).astype(o_ref.dtype)
        kf32 = k.astype(F32)
        kf = (kf32 * jnp.exp(lg_f * (t - 1.0 - pos))).T.astype(BF16)
        kb = (kf32 * jnp.exp(lg_b * pos)).T.astype(BF16)
        sf_ref[hh] = jnp.dot(kf, v, preferred_element_type=F32)
        sb_ref[hh] = jnp.dot(kb, v, preferred_element_type=F32)


def ctx_retention(p, seq, n_heads, lg, norm_g, heads=8):
    m = p.shape[0]
    nb = m // seq
    nhb = n_heads // heads
    w = heads * DK_C
    blk = lambda part: pl.BlockSpec((seq, w), lambda b, h, part=part: (b, part * nhb + h))
    state = pl.BlockSpec((None, None, heads, DK_C, DK_C), lambda b, h: (b, 0, h, 0, 0))
    return pl.pallas_call(
        functools.partial(_ctx_ret_kernel, heads=heads),
        grid=(nb, nhb),
        in_specs=[pl.BlockSpec(memory_space=pltpu.SMEM), blk(0), blk(1), blk(2), blk(3),
                  pl.BlockSpec((1, DK_C), lambda b, h: (0, 0))],
        out_specs=[pl.BlockSpec((seq, w), lambda b, h: (b, h)), state, state],
        out_shape=[jax.ShapeDtypeStruct((m, n_heads * DK_C), BF16),
                   jax.ShapeDtypeStruct((nb, 1, n_heads, DK_C, DK_C), F32),
                   jax.ShapeDtypeStruct((nb, 1, n_heads, DK_C, DK_C), F32)],
        compiler_params=_params("parallel", "parallel"),
        name="ctx_retention",
    )(lg, p, p, p, p, norm_g.reshape(1, DK_C))


def _rope_c(x, cos, sin):
    halves = []
    for c in range(DK_C // LANES):
        sl = slice(c * LANES, (c + 1) * LANES)
        xs = x[:, sl]
        halves.append(xs * cos[:, sl] + pltpu.roll(xs, LANES // 2, 1) * sin[:, sl])
    return jnp.concatenate(halves, axis=-1)


def _lat_ret_kernel(lg_ref, q_ref, k_ref, v_ref, g_ref, s0f_ref, s0b_ref, ng_ref,
                    qcos_ref, qsin_ref, kcos_ref, ksin_ref, o_ref, *, tq):
    h = pl.program_id(1)
    row0 = pl.program_id(2) * tq
    lg_f = lg_ref[0, h]
    lg_b = lg_ref[1, h]
    t = k_ref.shape[0]
    q = _rope_c(q_ref[...].astype(F32), qcos_ref[...], qsin_ref[...]).astype(BF16)
    k = _rope_c(k_ref[...].astype(F32) * (DK_C ** -0.5), kcos_ref[...], ksin_ref[...]).astype(BF16)
    s = lax.dot_general(q, k, NT_DIMS, preferred_element_type=F32)
    att = (s * _decay_mask(_decay_geometry(row0, tq, t), lg_f, lg_b)).astype(BF16)
    o = jnp.dot(att, v_ref[...], preferred_element_type=F32)
    pos = (lax.broadcasted_iota(jnp.int32, (tq, 1), 0) + row0).astype(F32)
    o += jnp.dot(q, s0f_ref[...].astype(BF16), preferred_element_type=F32) * jnp.exp(lg_f * (pos + 1.0))
    o += jnp.dot(q, s0b_ref[...].astype(BF16), preferred_element_type=F32) * jnp.exp(lg_b * (t - pos))
    o_ref[...] = _gated_head_norm(o, g_ref[...], ng_ref[...]).astype(o_ref.dtype)


def lat_retention(p, seq, n_heads, lg, norm_g, s0_f, s0_b, rope, tq=1024):
    m = p.shape[0]
    nb = m // seq
    nq = seq // tq
    qblk = lambda part: pl.BlockSpec((tq, DK_C), lambda b, h, i, part=part: (b * nq + i, part * n_heads + h))
    kblk = lambda part: pl.BlockSpec((seq, DK_C), lambda b, h, i, part=part: (b, part * n_heads + h))
    state = pl.BlockSpec((None, None, None, DK_C, DK_C), lambda b, h, i: (b, 0, h, 0, 0))
    qtab = pl.BlockSpec((tq, DK_C), lambda b, h, i: (i, 0))
    ktab = pl.BlockSpec((seq, DK_C), lambda b, h, i: (0, 0))
    cos, sin = rope
    return pl.pallas_call(
        functools.partial(_lat_ret_kernel, tq=tq),
        grid=(nb, n_heads, nq),
        in_specs=[pl.BlockSpec(memory_space=pltpu.SMEM), qblk(0), kblk(1), kblk(2), qblk(3),
                  state, state, pl.BlockSpec((1, DK_C), lambda b, h, i: (0, 0)),
                  qtab, qtab, ktab, ktab],
        out_specs=pl.BlockSpec((tq, DK_C), lambda b, h, i: (b * nq + i, h)),
        out_shape=jax.ShapeDtypeStruct((m, n_heads * DK_C), BF16),
        compiler_params=_params("parallel", "parallel", "parallel"),
        name="lat_retention",
    )(lg, p, p, p, p, s0_f, s0_b, norm_g.reshape(1, DK_C), cos, sin, cos, sin)


def _rope_tables(seq, width, group):
    half = group // 2
    t = np.arange(seq)
    rows, cols = t // GRID_W, t % GRID_W
    lane = np.arange(width)
    inv = (ROPE_BASE ** (-jnp.arange(half, dtype=F32) / half))[lane % half]
    use_cols = (lane // group) % 2 == 1
    pos = np.where(use_cols[None, :], cols[:, None], rows[:, None])
    ang = jnp.asarray(pos, dtype=jnp.int32).astype(F32) * inv[None, :]
    first = jnp.asarray((lane % group) < half)
    return jnp.cos(ang), jnp.sin(ang), first


def rope_tables_a(seq):
    cos, sin, first = _rope_tables(seq, D_HEAD, DQK_A // 2)
    return cos, jnp.where(first, -sin, 0.0), jnp.where(first, 0.0, sin)


def rope_tables_c(seq):
    cos, sin, first = _rope_tables(seq, DK_C, DK_C // 2)
    return cos, jnp.where(first, -sin, sin)


def kernel(x_prompt, x_sample, c, cache_a_k, cache_a_v, cache_b_k, cache_b_v, state_c_fwd, state_c_bwd,
           c_ctx, norm_mix_g, norm_mlp_g, w_mod, b_mod, w_mlp_in, w_mlp_out, w_in_ab, w_out_ab,
           diff_lambda, diff_subln_g, nat_rpb, w_in_c, w_out_c, ret_log_decay_fwd, ret_log_decay_bwd,
           ret_norm_g, final_norm_g):
    bp, seq_p, d = x_prompt.shape
    bs, seq_s, _ = x_sample.shape
    depth = w_mod.shape[0]
    n_heads_ab = w_out_ab.shape[1] // (2 * D_HEAD)
    n_heads_c = w_out_c.shape[1] // DK_C
    past = cache_a_k.shape[2]
    w_ab = n_heads_ab * D_HEAD

    xp = x_prompt.reshape(bp * seq_p, d)
    xs = x_sample.reshape(bs * seq_s, d)
    rows_p = bp * seq_p

    cvec = jnp.concatenate([c_ctx[None, :], c, jnp.zeros((MOD_ROWS - 1 - bs, d), F32)], axis=0)
    mod = mod_vectors(cvec, w_mod, b_mod).reshape(depth, MOD_ROWS, 6, 1, d)

    new_a_k, new_a_v, new_b_k, new_b_v, new_c_f, new_c_b = [], [], [], [], [], []

    for l in range(depth):
        sh1_p, sc1_p, g1_p, sh2_p, sc2_p, g2_p = (mod[l, 0:1, i] for i in range(6))
        sh1_s, sc1_s, g1_s, sh2_s, sc2_s, g2_s = (mod[l, 1:1 + bs, i] for i in range(6))
        hp = norm_modulate(xp, norm_mix_g[l], sh1_p, sc1_p, rows_p)
        hs = norm_modulate(xs, norm_mix_g[l], sh1_s, sc1_s, seq_s)

        if l % 2 == 0:
            e = l // 2
            lam_init = 0.8 - 0.6 * float(np.exp(-0.3 * l))
            w_out, w_out_layer = w_out_ab, e
            pp = matmul(hp, w_in_ab, e)
            ps = matmul(hs, w_in_ab, e)
            oa, ob, akt, av, bk, bv = ctx_ab_attention(pp, seq_p, n_heads_ab, diff_lambda[e],
                                                       diff_subln_g[e], lam_init)
            mix_p = [oa, ob]
            new_a_k.append(akt.reshape(bp, 2 * n_heads_ab, DQK_A, seq_p).transpose(0, 3, 1, 2))
            new_a_v.append(av.reshape(bp, seq_p, n_heads_ab, D_HEAD))
            new_b_k.append(bk.reshape(bp, seq_p, n_heads_ab, D_HEAD))
            new_b_v.append(bv.reshape(bp, seq_p, n_heads_ab, D_HEAD))
            bias = na_bias_table(nat_rpb[e], seq_s // GRID_W)
            mix_s = lat_ab_attention(
                ps, seq_s, n_heads_ab,
                cache_a_k[:, e].reshape(bs, past, w_ab), cache_a_v[:, e].reshape(bs, past, w_ab),
                cache_b_k[:, e].reshape(bs, past, w_ab), cache_b_v[:, e].reshape(bs, past, w_ab),
                bias, rope_tables_a(seq_s), diff_lambda[e], diff_subln_g[e], lam_init)
        else:
            o = l // 2
            w_out, w_out_layer = w_out_c, o
            lg = jnp.stack([ret_log_decay_fwd[o], ret_log_decay_bwd[o]])
            pp = matmul(hp, w_in_c, o, out_dtype=BF16)
            ps = matmul(hs, w_in_c, o, out_dtype=BF16)
            mix_p, s_f, s_b = ctx_retention(pp, seq_p, n_heads_c, lg, ret_norm_g[o])
            new_c_f.append(s_f)
            new_c_b.append(s_b)
            mix_s = lat_retention(ps, seq_s, n_heads_c, lg, ret_norm_g[o],
                                  state_c_fwd[:, o:o + 1], state_c_bwd[:, o:o + 1], rope_tables_c(seq_s))

        xp = matmul(mix_p, w_out, w_out_layer, mode="resgate", res=xp, gate=g1_p, rows_per_group=rows_p)
        xs = matmul(mix_s, w_out, w_out_layer, mode="resgate", res=xs, gate=g1_s, rows_per_group=seq_s)

        hp = norm_modulate(xp, norm_mlp_g[l], sh2_p, sc2_p, rows_p)
        hs = norm_modulate(xs, norm_mlp_g[l], sh2_s, sc2_s, seq_s)
        up = matmul(hp, w_mlp_in, l, mode="relu2", out_dtype=BF16)
        us = matmul(hs, w_mlp_in, l, mode="relu2", out_dtype=BF16)
        xp = matmul(up, w_mlp_out, l, mode="resgate", res=xp, gate=g2_p, rows_per_group=rows_p, tn=1024, tk=1024)
        xs = matmul(us, w_mlp_out, l, mode="resgate", res=xs, gate=g2_s, rows_per_group=seq_s, tn=1024, tk=1024)

    y_prompt = rms_norm(xp, final_norm_g).reshape(bp, seq_p, d)
    y_sample = rms_norm(xs, final_norm_g).reshape(bs, seq_s, d)
    return (y_prompt, y_sample,
            jnp.stack(new_a_k, axis=1), jnp.stack(new_a_v, axis=1),
            jnp.stack(new_b_k, axis=1), jnp.stack(new_b_v, axis=1),
            jnp.concatenate(new_c_f, axis=1), jnp.concatenate(new_c_b, axis=1))
```

```python
import functools

import jax
import jax.numpy as jnp
import numpy as np
from jax import lax
from jax.experimental import pallas as pl
from jax.experimental.pallas import tpu as pltpu

F32 = jnp.float32
BF16 = jnp.bfloat16

EPS = 1e-6
ROPE_BASE = 10000.0
GRID_W = 64
WIN_R = 8
WIN_C = 16
DQK_A = 64
D_HEAD = 128
DK_C = 256
LANES = 128
MOD_ROWS = 8
VMEM_LIMIT = 62 * 1024 * 1024

NT_DIMS = (((1,), (1,)), ((), ()))


def _params(*sem):
    return pltpu.CompilerParams(dimension_semantics=sem, vmem_limit_bytes=VMEM_LIMIT)


def _silu(x):
    return x / (1.0 + jnp.exp(-x))


def _modvec_kernel(c_ref, w_ref, b_ref, o_ref):
    s = _silu(c_ref[...]).astype(BF16)
    w = w_ref[0].astype(BF16)
    o_ref[0] = jnp.dot(s, w, preferred_element_type=F32) + b_ref[0]


def mod_vectors(cvec, w_mod, b_mod, tn=512):
    depth, d, n = w_mod.shape
    return pl.pallas_call(
        _modvec_kernel,
        grid=(depth, n // tn),
        in_specs=[
            pl.BlockSpec((MOD_ROWS, d), lambda l, j: (0, 0)),
            pl.BlockSpec((1, d, tn), lambda l, j: (l, 0, j)),
            pl.BlockSpec((1, 1, tn), lambda l, j: (l, 0, j)),
        ],
        out_specs=pl.BlockSpec((1, MOD_ROWS, tn), lambda l, j: (l, 0, j)),
        out_shape=jax.ShapeDtypeStruct((depth, MOD_ROWS, n), F32),
        compiler_params=_params("parallel", "parallel"),
        name="mod_vectors",
    )(cvec, w_mod, b_mod.reshape(depth, 1, n))


def _norm_mod_kernel(x_ref, g_ref, sh_ref, sc_ref, o_ref):
    x = x_ref[...]
    y = x * lax.rsqrt(jnp.mean(x * x, axis=-1, keepdims=True) + EPS) * g_ref[...]
    o_ref[...] = (y * (1.0 + sc_ref[0]) + sh_ref[0]).astype(o_ref.dtype)


def norm_modulate(x, g, shift, scale, rows_per_group, tr=512):
    m, d = x.shape
    grp = lambda i: ((i * tr) // rows_per_group, 0, 0)
    return pl.pallas_call(
        _norm_mod_kernel,
        grid=(m // tr,),
        in_specs=[
            pl.BlockSpec((tr, d), lambda i: (i, 0)),
            pl.BlockSpec((1, d), lambda i: (0, 0)),
            pl.BlockSpec((1, 1, d), grp),
            pl.BlockSpec((1, 1, d), grp),
        ],
        out_specs=pl.BlockSpec((tr, d), lambda i: (i, 0)),
        out_shape=jax.ShapeDtypeStruct((m, d), BF16),
        compiler_params=_params("parallel"),
        name="norm_modulate",
    )(x, g.reshape(1, d), shift, scale)


def _norm_kernel(x_ref, g_ref, o_ref):
    x = x_ref[...]
    o_ref[...] = x * lax.rsqrt(jnp.mean(x * x, axis=-1, keepdims=True) + EPS) * g_ref[...]


def rms_norm(x, g, tr=256):
    m, d = x.shape
    return pl.pallas_call(
        _norm_kernel,
        grid=(m // tr,),
        in_specs=[pl.BlockSpec((tr, d), lambda i: (i, 0)), pl.BlockSpec((1, d), lambda i: (0, 0))],
        out_specs=pl.BlockSpec((tr, d), lambda i: (i, 0)),
        out_shape=jax.ShapeDtypeStruct((m, d), F32),
        compiler_params=_params("parallel"),
        name="rms_norm",
    )(x, g.reshape(1, d))


def _mm_kernel(*refs, mode, nk, n_pairs, group_rows, emit_weights):
    x_refs = refs[:n_pairs]
    w_refs = refs[n_pairs:2 * n_pairs]
    refs = refs[2 * n_pairs:]
    if mode == "resgate":
        res_ref, gate_ref, o_ref = refs[:3]
        wb_refs = refs[3:]
    else:
        o_ref = refs[0]
        wb_refs = refs[1:]

    def product():
        acc = None
        for piece, (x_ref, w_ref) in enumerate(zip(x_refs, w_refs)):
            wb = w_ref[...].astype(BF16)
            if emit_weights:
                wb_refs[piece][...] = wb
            t = jnp.dot(x_ref[...], wb, preferred_element_type=F32)
            acc = t if acc is None else acc + t
        return acc

    def epilogue(acc):
        if mode == "relu2":
            r = jnp.maximum(acc, 0.0)
            o_ref[...] = (r * r).astype(o_ref.dtype)
        elif mode == "resgate":
            for g in range(gate_ref.shape[0]):
                rows = slice(g * group_rows, (g + 1) * group_rows)
                o_ref[rows, :] = res_ref[rows, :] + gate_ref[g] * acc[rows, :]
        else:
            o_ref[...] = acc.astype(o_ref.dtype)

    if nk == 1:
        epilogue(product())
        return

    k = pl.program_id(2)

    @pl.when(k == 0)
    def _():
        o_ref[...] = product()

    @pl.when(jnp.logical_and(k > 0, k < nk - 1))
    def _():
        o_ref[...] += product()

    @pl.when(k == nk - 1)
    def _():
        epilogue(o_ref[...] + product())


def matmul(xs, w, layer=None, *, mode="plain", out_dtype=F32, res=None, gate=None, rows_per_group=None,
           emit_weights=False, tm=2048, tn=512, tk=4096):
    xs = list(xs) if isinstance(xs, (list, tuple)) else [xs]
    n_pairs = len(xs)
    m, kpiece = xs[0].shape
    n = w[0].shape[1] if layer is None else w.shape[2]
    tm = min(tm, m)
    tk = min(tk, kpiece)
    nk = kpiece // tk
    assert n_pairs == 1 or nk == 1
    assert nk == 1 or out_dtype == F32
    assert not emit_weights or m == tm
    x_mode = dict(pipeline_mode=pl.Buffered(1)) if nk == 1 else {}
    in_specs = [pl.BlockSpec((tm, tk), lambda i, j, k: (i, k), **x_mode) for _ in xs]
    if layer is None:
        in_specs += [pl.BlockSpec((tk, tn), lambda i, j, k: (k, j)) for _ in xs]
        args = xs + list(w)
    else:
        in_specs += [pl.BlockSpec((None, tk, tn), lambda i, j, k, piece=piece: (layer, k + piece, j))
                     for piece in range(n_pairs)]
        args = xs + [w] * n_pairs
    group_rows = None
    if mode == "resgate":
        group_rows = min(rows_per_group, tm)
        groups_per_tile = tm // group_rows
        tiles_per_group = rows_per_group // group_rows
        in_specs += [
            pl.BlockSpec((tm, tn), lambda i, j, k: (i, j)),
            pl.BlockSpec((groups_per_tile, 1, tn), lambda i, j, k: (i // tiles_per_group, 0, j)),
        ]
        args += [res, gate]
    out_specs = [pl.BlockSpec((tm, tn), lambda i, j, k: (i, j))]
    out_shape = [jax.ShapeDtypeStruct((m, n), out_dtype)]
    if emit_weights:
        out_specs += [pl.BlockSpec((tk, tn), lambda i, j, k: (k, j)) for _ in xs]
        out_shape += [jax.ShapeDtypeStruct((kpiece, n), BF16) for _ in xs]
    outs = pl.pallas_call(
        functools.partial(_mm_kernel, mode=mode, nk=nk, n_pairs=n_pairs, group_rows=group_rows,
                          emit_weights=emit_weights),
        grid=(m // tm, n // tn, nk),
        in_specs=in_specs,
        out_specs=out_specs,
        out_shape=out_shape,
        compiler_params=_params("parallel", "parallel", "arbitrary"),
        name="matmul_" + mode,
    )(*args)
    return (outs[0], outs[1:]) if emit_weights else outs[0]


def _exp_parts(chains):
    ms = [functools.reduce(jnp.maximum, [jnp.max(s, axis=-1, keepdims=True) for s in c]) for c in chains]
    es = [[jnp.exp(s - m) for s in c] for c, m in zip(chains, ms)]
    dens = [functools.reduce(jnp.add, [jnp.sum(e, axis=-1, keepdims=True) for e in ec]) for ec in es]
    return [([e.astype(BF16) for e in ec], 1.0 / d) for ec, d in zip(es, dens)]


def _qk(q_bf, k_bf, keys_transposed=False):
    if keys_transposed:
        return jnp.dot(q_bf, k_bf, preferred_element_type=F32)
    return lax.dot_general(q_bf, k_bf, NT_DIMS, preferred_element_type=F32)


def _weighted_values(es, vs):
    o = None
    for e, v in zip(es, vs):
        t = jnp.dot(e, v, preferred_element_type=F32)
        o = t if o is None else o + t
    return o


def _diff_lambda(lp_ref, lam_init):
    lp = lp_ref[...]
    a = jnp.sum(lp[0:1] * lp[1:2], axis=-1, keepdims=True)
    b = jnp.sum(lp[2:3] * lp[3:4], axis=-1, keepdims=True)
    return jnp.exp(a) - jnp.exp(b) + lam_init


def _ab_scores(qa, kas, qb, kbs, biases, keys_transposed=False):
    lane = lax.broadcasted_iota(jnp.int32, qa.shape, 1)
    qs = qa * (DQK_A ** -0.5)
    q1 = jnp.where(lane < DQK_A, qs, 0.0).astype(BF16)
    q2 = jnp.where(lane >= DQK_A, qs, 0.0).astype(BF16)
    q3 = qb.astype(BF16)
    s3 = []
    for k, bias in zip(kbs, biases):
        s = _qk(q3, k) * (D_HEAD ** -0.5)
        s3.append(s if bias is None else s + bias)
    return ([_qk(q1, k, keys_transposed) for k in kas], [_qk(q2, k, keys_transposed) for k in kas], s3)


def _ab_values(parts, vas, vbs, lam, subln_g, lam_init):
    (e1, inv1), (e2, inv2), (e3, inv3) = parts
    o = _weighted_values(e1, vas) * inv1 - _weighted_values(e2, vas) * (lam * inv2)
    y = o * lax.rsqrt(jnp.mean(o * o, axis=-1, keepdims=True) + EPS) * subln_g
    return y * (1.0 - lam_init), _weighted_values(e3, vbs) * inv3


def _rope_a(x, cos, sin_lo, sin_hi):
    half = DQK_A // 4
    return x * cos + pltpu.roll(x, LANES - half, 1) * sin_lo + pltpu.roll(x, half, 1) * sin_hi


def _ctx_ab_kernel(qa_ref, ka_ref, va_ref, qb_ref, kb_ref, vb_ref, lp_ref, g_ref,
                   oa_ref, ob_ref, akt_ref, av_ref, bk_ref, bv_ref, *, lam_init, heads):
    lam = _diff_lambda(lp_ref, lam_init)
    seq = qa_ref.shape[0]
    akt_ref[0] = ka_ref[...].T
    sls = [slice(h * D_HEAD, (h + 1) * D_HEAD) for h in range(heads)]

    lane = lax.broadcasted_iota(jnp.int32, (seq, D_HEAD), 1)
    scale_a = DQK_A ** -0.5
    scale_b = D_HEAD ** -0.5

    def stage_scores(sl):
        k = akt_ref[0, sl, :].astype(BF16)
        q = qa_ref[:, sl]
        return (_qk(jnp.where(lane < DQK_A, q, 0.0).astype(BF16), k, True) * scale_a,
                _qk(jnp.where(lane >= DQK_A, q, 0.0).astype(BF16), k, True) * scale_a,
                _qk(qb_ref[:, sl].astype(BF16), kb_ref[:, sl].astype(BF16)) * scale_b)

    def stage_softmax(s):
        ms = [jnp.max(x, axis=-1, keepdims=True) for x in s]
        es = [jnp.exp(x - m) for x, m in zip(s, ms)]
        invs = [1.0 / jnp.sum(e, axis=-1, keepdims=True) for e in es]
        p1, p2, p3 = (e * i for e, i in zip(es, invs))
        return (p1 - lam * p2).astype(BF16), p3.astype(BF16)

    def stage_values(sl, p):
        o = jnp.dot(p[0], va_ref[:, sl].astype(BF16), preferred_element_type=F32)
        ob = jnp.dot(p[1], vb_ref[:, sl].astype(BF16), preferred_element_type=F32)
        y = o * lax.rsqrt(jnp.mean(o * o, axis=-1, keepdims=True) + EPS) * g_ref[...]
        oa_ref[:, sl] = (y * (1.0 - lam_init)).astype(oa_ref.dtype)
        ob_ref[:, sl] = ob.astype(ob_ref.dtype)

    scores, probs = {}, {}
    for t in range(heads + 2):
        if t < heads:
            scores[t] = stage_scores(sls[t])
        if 0 <= t - 1 < heads:
            probs[t - 1] = stage_softmax(scores.pop(t - 1))
        if 0 <= t - 2 < heads:
            stage_values(sls[t - 2], probs.pop(t - 2))
    av_ref[...] = va_ref[...].reshape(seq, heads, D_HEAD)
    bk_ref[...] = kb_ref[...].reshape(seq, heads, D_HEAD)
    bv_ref[...] = vb_ref[...].reshape(seq, heads, D_HEAD)


def ctx_ab_attention(p, seq, n_heads, diff_lambda, subln_g, lam_init, heads=8):
    m = p.shape[0]
    nb = m // seq
    nhb = n_heads // heads
    w = heads * D_HEAD
    blk = lambda part: pl.BlockSpec((seq, w), lambda b, h, part=part: (b, part * nhb + h))
    out2d = pl.BlockSpec((seq, w), lambda b, h: (b, h))
    cache = pl.BlockSpec((seq, heads, D_HEAD), lambda b, h: (b, h, 0))
    return pl.pallas_call(
        functools.partial(_ctx_ab_kernel, lam_init=lam_init, heads=heads),
        grid=(nb, nhb),
        in_specs=[blk(0), blk(1), blk(2), blk(3), blk(4), blk(5),
                  pl.BlockSpec((4, DQK_A), lambda b, h: (0, 0)),
                  pl.BlockSpec((1, D_HEAD), lambda b, h: (0, 0))],
        out_specs=[out2d, out2d, pl.BlockSpec((1, w, seq), lambda b, h: (b, h, 0)), cache, cache, cache],
        out_shape=[jax.ShapeDtypeStruct((m, n_heads * D_HEAD), BF16),
                   jax.ShapeDtypeStruct((m, n_heads * D_HEAD), BF16),
                   jax.ShapeDtypeStruct((nb, n_heads * D_HEAD, seq), F32),
                   jax.ShapeDtypeStruct((m, n_heads, D_HEAD), F32),
                   jax.ShapeDtypeStruct((m, n_heads, D_HEAD), F32),
                   jax.ShapeDtypeStruct((m, n_heads, D_HEAD), F32)],
        compiler_params=_params("parallel", "parallel"),
        name="ctx_ab_attention",
    )(p, p, p, p, p, p, diff_lambda, subln_g.reshape(1, D_HEAD))


def _na_bias_kernel(rpb_ref, o_ref, *, n_rows):
    h = pl.program_id(0)
    nd = 2 * WIN_C - 1
    ne = 2 * WIN_R - 1
    shape = (GRID_W, LANES)
    lane = lax.broadcasted_iota(jnp.int32, shape, 1)
    qc = lax.broadcasted_iota(jnp.int32, shape, 0)
    kc = lane & (GRID_W - 1)
    diff = kc - qc + (WIN_C - 1)
    cs = jnp.clip(qc - WIN_C // 2, 0, GRID_W - WIN_C)
    col_ok = (kc >= cs) & (kc < cs + WIN_C)
    low_half = lane < GRID_W
    neg = jnp.full(shape, -jnp.inf, F32)

    toeplitz = []
    for e in range(ne):
        acc = jnp.zeros(shape, F32)
        for d in range(nd):
            acc = jnp.where(diff == d, rpb_ref[h, e * nd + d], acc)
        toeplitz.append(jnp.where(col_ok, acc, neg))

    wr = min(WIN_R, n_rows)
    for r in range(n_rows):
        rs = min(max(r - wr // 2, 0), n_rows - wr)
        for pair in range(n_rows // 2):
            kr0, kr1 = 2 * pair, 2 * pair + 1
            a = toeplitz[kr0 - r + WIN_R - 1] if rs <= kr0 < rs + wr else neg
            b = toeplitz[kr1 - r + WIN_R - 1] if rs <= kr1 < rs + wr else neg
            blk = a if a is b else jnp.where(low_half, a, b)
            o_ref[0, r * GRID_W:(r + 1) * GRID_W, pair * LANES:(pair + 1) * LANES] = blk


def na_bias_table(rpb, n_rows):
    n_heads = rpb.shape[0]
    t = n_rows * GRID_W
    return pl.pallas_call(
        functools.partial(_na_bias_kernel, n_rows=n_rows),
        grid=(n_heads,),
        in_specs=[pl.BlockSpec(memory_space=pltpu.SMEM)],
        out_specs=pl.BlockSpec((1, t, t), lambda h: (h, 0, 0)),
        out_shape=jax.ShapeDtypeStruct((n_heads, t, t), F32),
        compiler_params=_params("parallel"),
        name="na_bias_table",
    )(rpb.reshape(n_heads, -1))


def _lat_ab_kernel(qa_ref, ka_ref, va_ref, qb_ref, kb_ref, vb_ref,
                   cak_ref, cav_ref, cbk_ref, cbv_ref, bias_ref,
                   qcos_ref, qslo_ref, qshi_ref, kcos_ref, kslo_ref, kshi_ref,
                   lp_ref, g_ref, oa_ref, ob_ref, *, lam_init):
    lam = _diff_lambda(lp_ref, lam_init)
    q = _rope_a(qa_ref[...], qcos_ref[...], qslo_ref[...], qshi_ref[...])
    k = _rope_a(ka_ref[...], kcos_ref[...], kslo_ref[...], kshi_ref[...])
    scores = _ab_scores(q, [k.astype(BF16), cak_ref[0].astype(BF16)],
                        qb_ref[...], [kb_ref[...].astype(BF16), cbk_ref[0].astype(BF16)], [bias_ref[0], None])
    parts = _exp_parts(scores)
    oa, ob = _ab_values(parts, [va_ref[...].astype(BF16), cav_ref[0].astype(BF16)],
                        [vb_ref[...].astype(BF16), cbv_ref[0].astype(BF16)], lam, g_ref[...], lam_init)
    oa_ref[...] = oa.astype(oa_ref.dtype)
    ob_ref[...] = ob.astype(ob_ref.dtype)


def lat_ab_attention(p, seq, n_heads, cache_a_k, cache_a_v, cache_b_k, cache_b_v, bias, rope,
                     diff_lambda, subln_g, lam_init, tq=1024):
    m = p.shape[0]
    nb = m // seq
    nq = seq // tq
    past = cache_a_k.shape[1]
    qblk = lambda part: pl.BlockSpec((tq, D_HEAD), lambda b, h, i, part=part: (b * nq + i, part * n_heads + h))
    kblk = lambda part: pl.BlockSpec((seq, D_HEAD), lambda b, h, i, part=part: (b, part * n_heads + h))
    cblk = pl.BlockSpec((1, past, D_HEAD), lambda b, h, i: (b, 0, h))
    qtab = pl.BlockSpec((tq, D_HEAD), lambda b, h, i: (i, 0))
    ktab = pl.BlockSpec((seq, D_HEAD), lambda b, h, i: (0, 0))
    cos, sin_lo, sin_hi = rope
    return pl.pallas_call(
        functools.partial(_lat_ab_kernel, lam_init=lam_init),
        grid=(nb, n_heads, nq),
        in_specs=[qblk(0), kblk(1), kblk(2), qblk(3), kblk(4), kblk(5),
                  cblk, cblk, cblk, cblk,
                  pl.BlockSpec((1, tq, seq), lambda b, h, i: (h, i, 0)),
                  qtab, qtab, qtab, ktab, ktab, ktab,
                  pl.BlockSpec((4, DQK_A), lambda b, h, i: (0, 0)),
                  pl.BlockSpec((1, D_HEAD), lambda b, h, i: (0, 0))],
        out_specs=[pl.BlockSpec((tq, D_HEAD), lambda b, h, i: (b * nq + i, h))] * 2,
        out_shape=[jax.ShapeDtypeStruct((m, n_heads * D_HEAD), BF16)] * 2,
        compiler_params=_params("parallel", "parallel", "parallel"),
        name="lat_ab_attention",
    )(p, p, p, p, p, p, cache_a_k, cache_a_v, cache_b_k, cache_b_v, bias,
      cos, sin_lo, sin_hi, cos, sin_lo, sin_hi, diff_lambda, subln_g.reshape(1, D_HEAD))


def _decay_geometry(row0, tq, tk):
    i = lax.broadcasted_iota(jnp.int32, (tq, tk), 0) + row0
    j = lax.broadcasted_iota(jnp.int32, (tq, tk), 1)
    rel = i - j
    return jnp.abs(rel).astype(F32), rel >= 0, jnp.where(rel == 0, 2.0, 1.0)


def _decay_mask(geometry, lg_f, lg_b):
    dist, is_fwd, diag = geometry
    return jnp.exp(jnp.where(is_fwd, lg_f, lg_b) * dist) * diag


def _gated_head_norm(o, g, norm_g):
    y = o * lax.rsqrt(jnp.mean(o * o, axis=-1, keepdims=True) + EPS) * norm_g
    return _silu(g.astype(F32)) * y


def _ctx_ret_kernel(lg_ref, q_ref, k_ref, v_ref, g_ref, ng_ref, o_ref, sf_ref, sb_ref, *, heads):
    t = q_ref.shape[0]
    pos = lax.broadcasted_iota(jnp.int32, (t, 1), 0).astype(F32)
    geometry = _decay_geometry(0, t, t)
    for hh in range(heads):
        h = pl.program_id(1) * heads + hh
        sl = slice(hh * DK_C, (hh + 1) * DK_C)
        lg_f = lg_ref[0, h]
        lg_b = lg_ref[1, h]
        k = k_ref[:, sl] * (DK_C ** -0.5)
        v = v_ref[:, sl]
        s = lax.dot_general(q_ref[:, sl], k, NT_DIMS, preferred_element_type=F32)
        att = (s * _decay_mask(geometry, lg_f, lg_b)).astype(BF16)
        o = jnp.dot(att, v, preferred_element_type=F32)
        o_ref[:, sl] = _gated_head_norm(o, g_ref[:, sl], ng_ref[...]).astype(o_ref.dtype)
        kf32 = k.astype(F32)
        kf = (kf32 * jnp.exp(lg_f * (t - 1.0 - pos))).T.astype(BF16)
        kb = (kf32 * jnp.exp(lg_b * pos)).T.astype(BF16)
        sf_ref[hh] = jnp.dot(kf, v, preferred_element_type=F32)
        sb_ref[hh] = jnp.dot(kb, v, preferred_element_type=F32)


def ctx_retention(p, seq, n_heads, lg, norm_g, heads=8):
    m = p.shape[0]
    nb = m // seq
    nhb = n_heads // heads
    w = heads * DK_C
    blk = lambda part: pl.BlockSpec((seq, w), lambda b, h, part=part: (b, part * nhb + h))
    state = pl.BlockSpec((None, None, heads, DK_C, DK_C), lambda b, h: (b, 0, h, 0, 0))
    return pl.pallas_call(
        functools.partial(_ctx_ret_kernel, heads=heads),
        grid=(nb, nhb),
        in_specs=[pl.BlockSpec(memory_space=pltpu.SMEM), blk(0), blk(1), blk(2), blk(3),
                  pl.BlockSpec((1, DK_C), lambda b, h: (0, 0))],
        out_specs=[pl.BlockSpec((seq, w), lambda b, h: (b, h)), state, state],
        out_shape=[jax.ShapeDtypeStruct((m, n_heads * DK_C), BF16),
                   jax.ShapeDtypeStruct((nb, 1, n_heads, DK_C, DK_C), F32),
                   jax.ShapeDtypeStruct((nb, 1, n_heads, DK_C, DK_C), F32)],
        compiler_params=_params("parallel", "parallel"),
        name="ctx_retention",
    )(lg, p, p, p, p, norm_g.reshape(1, DK_C))


def _rope_c(x, cos, sin):
    halves = []
    for c in range(DK_C // LANES):
        sl = slice(c * LANES, (c + 1) * LANES)
        xs = x[:, sl]
        halves.append(xs * cos[:, sl] + pltpu.roll(xs, LANES // 2, 1) * sin[:, sl])
    return jnp.concatenate(halves, axis=-1)


def _lat_ret_kernel(lg_ref, q_ref, k_ref, v_ref, g_ref, s0f_ref, s0b_ref, ng_ref,
                    qcos_ref, qsin_ref, kcos_ref, ksin_ref, o_ref, *, tq):
    h = pl.program_id(1)
    row0 = pl.program_id(2) * tq
    lg_f = lg_ref[0, h]
    lg_b = lg_ref[1, h]
    t = k_ref.shape[0]
    q = _rope_c(q_ref[...].astype(F32), qcos_ref[...], qsin_ref[...]).astype(BF16)
    k = _rope_c(k_ref[...].astype(F32) * (DK_C ** -0.5), kcos_ref[...], ksin_ref[...]).astype(BF16)
    s = lax.dot_general(q, k, NT_DIMS, preferred_element_type=F32)
    att = (s * _decay_mask(_decay_geometry(row0, tq, t), lg_f, lg_b)).astype(BF16)
    o = jnp.dot(att, v_ref[...], preferred_element_type=F32)
    pos = (lax.broadcasted_iota(jnp.int32, (tq, 1), 0) + row0).astype(F32)
    o += jnp.dot(q, s0f_ref[...].astype(BF16), preferred_element_type=F32) * jnp.exp(lg_f * (pos + 1.0))
    o += jnp.dot(q, s0b_ref[...].astype(BF16), preferred_element_type=F32) * jnp.exp(lg_b * (t - pos))
    o_ref[...] = _gated_head_norm(o, g_ref[...], ng_ref[...]).astype(o_ref.dtype)


def lat_retention(p, seq, n_heads, lg, norm_g, s0_f, s0_b, rope, tq=1024):
    m = p.shape[0]
    nb = m // seq
    nq = seq // tq
    qblk = lambda part: pl.BlockSpec((tq, DK_C), lambda b, h, i, part=part: (b * nq + i, part * n_heads + h))
    kblk = lambda part: pl.BlockSpec((seq, DK_C), lambda b, h, i, part=part: (b, part * n_heads + h))
    state = pl.BlockSpec((None, None, None, DK_C, DK_C), lambda b, h, i: (b, 0, h, 0, 0))
    qtab = pl.BlockSpec((tq, DK_C), lambda b, h, i: (i, 0))
    ktab = pl.BlockSpec((seq, DK_C), lambda b, h, i: (0, 0))
    cos, sin = rope
    return pl.pallas_call(
        functools.partial(_lat_ret_kernel, tq=tq),
        grid=(nb, n_heads, nq),
        in_specs=[pl.BlockSpec(memory_space=pltpu.SMEM), qblk(0), kblk(1), kblk(2), qblk(3),
                  state, state, pl.BlockSpec((1, DK_C), lambda b, h, i: (0, 0)),
                  qtab, qtab, ktab, ktab],
        out_specs=pl.BlockSpec((tq, DK_C), lambda b, h, i: (b * nq + i, h)),
        out_shape=jax.ShapeDtypeStruct((m, n_heads * DK_C), BF16),
        compiler_params=_params("parallel", "parallel", "parallel"),
        name="lat_retention",
    )(lg, p, p, p, p, s0_f, s0_b, norm_g.reshape(1, DK_C), cos, sin, cos, sin)


def _rope_tables(seq, width, group):
    half = group // 2
    t = np.arange(seq)
    rows, cols = t // GRID_W, t % GRID_W
    lane = np.arange(width)
    inv = (ROPE_BASE ** (-jnp.arange(half, dtype=F32) / half))[lane % half]
    use_cols = (lane // group) % 2 == 1
    pos = np.where(use_cols[None, :], cols[:, None], rows[:, None])
    ang = jnp.asarray(pos, dtype=jnp.int32).astype(F32) * inv[None, :]
    first = jnp.asarray((lane % group) < half)
    return jnp.cos(ang), jnp.sin(ang), first


def rope_tables_a(seq):
    cos, sin, first = _rope_tables(seq, D_HEAD, DQK_A // 2)
    return cos, jnp.where(first, -sin, 0.0), jnp.where(first, 0.0, sin)


def rope_tables_c(seq):
    cos, sin, first = _rope_tables(seq, DK_C, DK_C // 2)
    return cos, jnp.where(first, -sin, sin)


def kernel(x_prompt, x_sample, c, cache_a_k, cache_a_v, cache_b_k, cache_b_v, state_c_fwd, state_c_bwd,
           c_ctx, norm_mix_g, norm_mlp_g, w_mod, b_mod, w_mlp_in, w_mlp_out, w_in_ab, w_out_ab,
           diff_lambda, diff_subln_g, nat_rpb, w_in_c, w_out_c, ret_log_decay_fwd, ret_log_decay_bwd,
           ret_norm_g, final_norm_g):
    bp, seq_p, d = x_prompt.shape
    bs, seq_s, _ = x_sample.shape
    depth = w_mod.shape[0]
    n_heads_ab = w_out_ab.shape[1] // (2 * D_HEAD)
    n_heads_c = w_out_c.shape[1] // DK_C
    past = cache_a_k.shape[2]
    w_ab = n_heads_ab * D_HEAD

    xp = x_prompt.reshape(bp * seq_p, d)
    xs = x_sample.reshape(bs * seq_s, d)
    rows_p = bp * seq_p

    cvec = jnp.concatenate([c_ctx[None, :], c, jnp.zeros((MOD_ROWS - 1 - bs, d), F32)], axis=0)
    mod = mod_vectors(cvec, w_mod, b_mod).reshape(depth, MOD_ROWS, 6, 1, d)

    new_a_k, new_a_v, new_b_k, new_b_v, new_c_f, new_c_b = [], [], [], [], [], []

    for l in range(depth):
        sh1_p, sc1_p, g1_p, sh2_p, sc2_p, g2_p = (mod[l, 0:1, i] for i in range(6))
        sh1_s, sc1_s, g1_s, sh2_s, sc2_s, g2_s = (mod[l, 1:1 + bs, i] for i in range(6))
        hp = norm_modulate(xp, norm_mix_g[l], sh1_p, sc1_p, rows_p)
        hs = norm_modulate(xs, norm_mix_g[l], sh1_s, sc1_s, seq_s)

        if l % 2 == 0:
            e = l // 2
            lam_init = 0.8 - 0.6 * float(np.exp(-0.3 * l))
            w_out, w_out_layer = w_out_ab, e
            ps, w_in = matmul(hs, w_in_ab, e, emit_weights=True)
            pp = matmul(hp, w_in)
            oa, ob, akt, av, bk, bv = ctx_ab_attention(pp, seq_p, n_heads_ab, diff_lambda[e],
                                                       diff_subln_g[e], lam_init)
            mix_p = [oa, ob]
            new_a_k.append(akt.reshape(bp, 2 * n_heads_ab, DQK_A, seq_p).transpose(0, 3, 1, 2))
            new_a_v.append(av.reshape(bp, seq_p, n_heads_ab, D_HEAD))
            new_b_k.append(bk.reshape(bp, seq_p, n_heads_ab, D_HEAD))
            new_b_v.append(bv.reshape(bp, seq_p, n_heads_ab, D_HEAD))
            bias = na_bias_table(nat_rpb[e], seq_s // GRID_W)
            mix_s = lat_ab_attention(
                ps, seq_s, n_heads_ab,
                cache_a_k[:, e].reshape(bs, past, w_ab), cache_a_v[:, e].reshape(bs, past, w_ab),
                cache_b_k[:, e].reshape(bs, past, w_ab), cache_b_v[:, e].reshape(bs, past, w_ab),
                bias, rope_tables_a(seq_s), diff_lambda[e], diff_subln_g[e], lam_init)
        else:
            o = l // 2
            w_out, w_out_layer = w_out_c, o
            lg = jnp.stack([ret_log_decay_fwd[o], ret_log_decay_bwd[o]])
            ps, w_in = matmul(hs, w_in_c, o, out_dtype=BF16, emit_weights=True)
            pp = matmul(hp, w_in, out_dtype=BF16, tn=1024)
            mix_p, s_f, s_b = ctx_retention(pp, seq_p, n_heads_c, lg, ret_norm_g[o])
            new_c_f.append(s_f)
            new_c_b.append(s_b)
            mix_s = lat_retention(ps, seq_s, n_heads_c, lg, ret_norm_g[o],
                                  state_c_fwd[:, o:o + 1], state_c_bwd[:, o:o + 1], rope_tables_c(seq_s))

        xs, w_o = matmul(mix_s, w_out, w_out_layer, mode="resgate", res=xs, gate=g1_s, rows_per_group=seq_s,
                         emit_weights=True, tn=256)
        xp = matmul(mix_p, w_o, mode="resgate", res=xp, gate=g1_p, rows_per_group=rows_p)

        hp = norm_modulate(xp, norm_mlp_g[l], sh2_p, sc2_p, rows_p)
        hs = norm_modulate(xs, norm_mlp_g[l], sh2_s, sc2_s, seq_s)
        us, w1 = matmul(hs, w_mlp_in, l, mode="relu2", out_dtype=BF16, emit_weights=True)
        up = matmul(hp, w1, mode="relu2", out_dtype=BF16, tn=1024)
        xs, w2 = matmul(us, w_mlp_out, l, mode="resgate", res=xs, gate=g2_s, rows_per_group=seq_s,
                        emit_weights=True, tn=1024, tk=1024)
        xp = matmul(up, w2, mode="resgate", res=xp, gate=g2_p, rows_per_group=rows_p, tn=1024, tk=1024)

    y_prompt = rms_norm(xp, final_norm_g).reshape(bp, seq_p, d)
    y_sample = rms_norm(xs, final_norm_g).reshape(bs, seq_s, d)
    return (y_prompt, y_sample,
            jnp.stack(new_a_k, axis=1), jnp.stack(new_a_v, axis=1),
            jnp.stack(new_b_k, axis=1), jnp.stack(new_b_v, axis=1),
            jnp.concatenate(new_c_f, axis=1), jnp.concatenate(new_c_b, axis=1))
```

```python
import functools

import jax
import jax.numpy as jnp
import numpy as np
from jax import lax
from jax.experimental import pallas as pl
from jax.experimental.pallas import tpu as pltpu

F32 = jnp.float32
BF16 = jnp.bfloat16

EPS = 1e-6
ROPE_BASE = 10000.0
GRID_W = 64
WIN_R = 8
WIN_C = 16
DQK_A = 64
D_HEAD = 128
DK_C = 256
LANES = 128
MOD_ROWS = 8
VMEM_LIMIT = 62 * 1024 * 1024

NT_DIMS = (((1,), (1,)), ((), ()))


def _params(*sem):
    return pltpu.CompilerParams(dimension_semantics=sem, vmem_limit_bytes=VMEM_LIMIT)


def _silu(x):
    return x / (1.0 + jnp.exp(-x))


def _modvec_kernel(c_ref, w_ref, b_ref, o_ref):
    s = _silu(c_ref[...]).astype(BF16)
    w = w_ref[0].astype(BF16)
    o_ref[0] = jnp.dot(s, w, preferred_element_type=F32) + b_ref[0]


def mod_vectors(cvec, w_mod, b_mod, tn=512):
    depth, d, n = w_mod.shape
    return pl.pallas_call(
        _modvec_kernel,
        grid=(depth, n // tn),
        in_specs=[
            pl.BlockSpec((MOD_ROWS, d), lambda l, j: (0, 0)),
            pl.BlockSpec((1, d, tn), lambda l, j: (l, 0, j)),
            pl.BlockSpec((1, 1, tn), lambda l, j: (l, 0, j)),
        ],
        out_specs=pl.BlockSpec((1, MOD_ROWS, tn), lambda l, j: (l, 0, j)),
        out_shape=jax.ShapeDtypeStruct((depth, MOD_ROWS, n), F32),
        compiler_params=_params("parallel", "parallel"),
        name="mod_vectors",
    )(cvec, w_mod, b_mod.reshape(depth, 1, n))


def _norm_mod_kernel(x_ref, g_ref, sh_ref, sc_ref, o_ref):
    x = x_ref[...]
    y = x * lax.rsqrt(jnp.mean(x * x, axis=-1, keepdims=True) + EPS) * g_ref[...]
    o_ref[...] = (y * (1.0 + sc_ref[0]) + sh_ref[0]).astype(o_ref.dtype)


def norm_modulate(x, g, shift, scale, rows_per_group, tr=512):
    m, d = x.shape
    grp = lambda i: ((i * tr) // rows_per_group, 0, 0)
    return pl.pallas_call(
        _norm_mod_kernel,
        grid=(m // tr,),
        in_specs=[
            pl.BlockSpec((tr, d), lambda i: (i, 0)),
            pl.BlockSpec((1, d), lambda i: (0, 0)),
            pl.BlockSpec((1, 1, d), grp),
            pl.BlockSpec((1, 1, d), grp),
        ],
        out_specs=pl.BlockSpec((tr, d), lambda i: (i, 0)),
        out_shape=jax.ShapeDtypeStruct((m, d), BF16),
        compiler_params=_params("parallel"),
        name="norm_modulate",
    )(x, g.reshape(1, d), shift, scale)


def _norm_kernel(x_ref, g_ref, o_ref):
    x = x_ref[...]
    o_ref[...] = x * lax.rsqrt(jnp.mean(x * x, axis=-1, keepdims=True) + EPS) * g_ref[...]


def rms_norm(x, g, tr=256):
    m, d = x.shape
    return pl.pallas_call(
        _norm_kernel,
        grid=(m // tr,),
        in_specs=[pl.BlockSpec((tr, d), lambda i: (i, 0)), pl.BlockSpec((1, d), lambda i: (0, 0))],
        out_specs=pl.BlockSpec((tr, d), lambda i: (i, 0)),
        out_shape=jax.ShapeDtypeStruct((m, d), F32),
        compiler_params=_params("parallel"),
        name="rms_norm",
    )(x, g.reshape(1, d))


def _mm_kernel(*refs, mode, nk, n_pairs, group_rows, emit_weights):
    x_refs = refs[:n_pairs]
    w_refs = refs[n_pairs:2 * n_pairs]
    refs = refs[2 * n_pairs:]
    if mode == "resgate":
        res_ref, gate_ref, o_ref = refs[:3]
        wb_refs = refs[3:]
    else:
        o_ref = refs[0]
        wb_refs = refs[1:]

    def product():
        acc = None
        for piece, (x_ref, w_ref) in enumerate(zip(x_refs, w_refs)):
            wb = w_ref[...].astype(BF16)
            if emit_weights:
                wb_refs[piece][...] = wb
            t = jnp.dot(x_ref[...], wb, preferred_element_type=F32)
            acc = t if acc is None else acc + t
        return acc

    def epilogue(acc):
        if mode == "relu2":
            r = jnp.maximum(acc, 0.0)
            o_ref[...] = (r * r).astype(o_ref.dtype)
        elif mode == "resgate":
            for g in range(gate_ref.shape[0]):
                rows = slice(g * group_rows, (g + 1) * group_rows)
                o_ref[rows, :] = res_ref[rows, :] + gate_ref[g] * acc[rows, :]
        else:
            o_ref[...] = acc.astype(o_ref.dtype)

    if nk == 1:
        epilogue(product())
        return

    k = pl.program_id(2)

    @pl.when(k == 0)
    def _():
        o_ref[...] = product()

    @pl.when(jnp.logical_and(k > 0, k < nk - 1))
    def _():
        o_ref[...] += product()

    @pl.when(k == nk - 1)
    def _():
        epilogue(o_ref[...] + product())


def matmul(xs, w, layer=None, *, mode="plain", out_dtype=F32, res=None, gate=None, rows_per_group=None,
           emit_weights=False, tm=2048, tn=512, tk=4096):
    xs = list(xs) if isinstance(xs, (list, tuple)) else [xs]
    n_pairs = len(xs)
    m, kpiece = xs[0].shape
    n = w[0].shape[1] if layer is None else w.shape[2]
    tm = min(tm, m)
    tk = min(tk, kpiece)
    nk = kpiece // tk
    assert n_pairs == 1 or nk == 1
    assert nk == 1 or out_dtype == F32
    assert not emit_weights or m == tm
    x_mode = dict(pipeline_mode=pl.Buffered(1)) if nk == 1 else {}
    in_specs = [pl.BlockSpec((tm, tk), lambda i, j, k: (i, k), **x_mode) for _ in xs]
    if layer is None:
        in_specs += [pl.BlockSpec((tk, tn), lambda i, j, k: (k, j)) for _ in xs]
        args = xs + list(w)
    else:
        in_specs += [pl.BlockSpec((None, tk, tn), lambda i, j, k, piece=piece: (layer, k + piece, j))
                     for piece in range(n_pairs)]
        args = xs + [w] * n_pairs
    group_rows = None
    if mode == "resgate":
        group_rows = min(rows_per_group, tm)
        groups_per_tile = tm // group_rows
        tiles_per_group = rows_per_group // group_rows
        in_specs += [
            pl.BlockSpec((tm, tn), lambda i, j, k: (i, j)),
            pl.BlockSpec((groups_per_tile, 1, tn), lambda i, j, k: (i // tiles_per_group, 0, j)),
        ]
        args += [res, gate]
    out_specs = [pl.BlockSpec((tm, tn), lambda i, j, k: (i, j))]
    out_shape = [jax.ShapeDtypeStruct((m, n), out_dtype)]
    if emit_weights:
        out_specs += [pl.BlockSpec((tk, tn), lambda i, j, k: (k, j)) for _ in xs]
        out_shape += [jax.ShapeDtypeStruct((kpiece, n), BF16) for _ in xs]
    outs = pl.pallas_call(
        functools.partial(_mm_kernel, mode=mode, nk=nk, n_pairs=n_pairs, group_rows=group_rows,
                          emit_weights=emit_weights),
        grid=(m // tm, n // tn, nk),
        in_specs=in_specs,
        out_specs=out_specs,
        out_shape=out_shape,
        compiler_params=_params("parallel", "parallel", "arbitrary"),
        name="matmul_" + mode,
    )(*args)
    return (outs[0], outs[1:]) if emit_weights else outs[0]


def _exp_parts(chains):
    ms = [functools.reduce(jnp.maximum, [jnp.max(s, axis=-1, keepdims=True) for s in c]) for c in chains]
    es = [[jnp.exp(s - m) for s in c] for c, m in zip(chains, ms)]
    dens = [functools.reduce(jnp.add, [jnp.sum(e, axis=-1, keepdims=True) for e in ec]) for ec in es]
    return [([e.astype(BF16) for e in ec], 1.0 / d) for ec, d in zip(es, dens)]


def _qk(q_bf, k_bf, keys_transposed=False):
    if keys_transposed:
        return jnp.dot(q_bf, k_bf, preferred_element_type=F32)
    return lax.dot_general(q_bf, k_bf, NT_DIMS, preferred_element_type=F32)


def _weighted_values(es, vs):
    o = None
    for e, v in zip(es, vs):
        t = jnp.dot(e, v, preferred_element_type=F32)
        o = t if o is None else o + t
    return o


def _diff_lambda(lp_ref, lam_init):
    lp = lp_ref[...]
    a = jnp.sum(lp[0:1] * lp[1:2], axis=-1, keepdims=True)
    b = jnp.sum(lp[2:3] * lp[3:4], axis=-1, keepdims=True)
    return jnp.exp(a) - jnp.exp(b) + lam_init


def _ab_scores(qa, kas, qb, kbs, biases, keys_transposed=False):
    lane = lax.broadcasted_iota(jnp.int32, qa.shape, 1)
    qs = qa * (DQK_A ** -0.5)
    q1 = jnp.where(lane < DQK_A, qs, 0.0).astype(BF16)
    q2 = jnp.where(lane >= DQK_A, qs, 0.0).astype(BF16)
    q3 = qb.astype(BF16)
    s3 = []
    for k, bias in zip(kbs, biases):
        s = _qk(q3, k) * (D_HEAD ** -0.5)
        s3.append(s if bias is None else s + bias)
    return ([_qk(q1, k, keys_transposed) for k in kas], [_qk(q2, k, keys_transposed) for k in kas], s3)


def _ab_values(parts, vas, vbs, lam, subln_g, lam_init):
    (e1, inv1), (e2, inv2), (e3, inv3) = parts
    o = _weighted_values(e1, vas) * inv1 - _weighted_values(e2, vas) * (lam * inv2)
    y = o * lax.rsqrt(jnp.mean(o * o, axis=-1, keepdims=True) + EPS) * subln_g
    return y * (1.0 - lam_init), _weighted_values(e3, vbs) * inv3


def _rope_a(x, cos, sin_lo, sin_hi):
    half = DQK_A // 4
    return x * cos + pltpu.roll(x, LANES - half, 1) * sin_lo + pltpu.roll(x, half, 1) * sin_hi


def _ctx_ab_kernel(qa_ref, ka_ref, va_ref, qb_ref, kb_ref, vb_ref, lp_ref, g_ref,
                   oa_ref, ob_ref, akt_ref, av_ref, bk_ref, bv_ref, *, lam_init, heads):
    lam = _diff_lambda(lp_ref, lam_init)
    seq = qa_ref.shape[0]
    akt_ref[0] = ka_ref[...].T
    sls = [slice(h * D_HEAD, (h + 1) * D_HEAD) for h in range(heads)]

    lane = lax.broadcasted_iota(jnp.int32, (seq, D_HEAD), 1)
    scale_a = DQK_A ** -0.5
    scale_b = D_HEAD ** -0.5

    def stage_scores(sl):
        k = akt_ref[0, sl, :].astype(BF16)
        q = qa_ref[:, sl]
        return (_qk(jnp.where(lane < DQK_A, q, 0.0).astype(BF16), k, True) * scale_a,
                _qk(jnp.where(lane >= DQK_A, q, 0.0).astype(BF16), k, True) * scale_a,
                _qk(qb_ref[:, sl].astype(BF16), kb_ref[:, sl].astype(BF16)) * scale_b)

    def stage_softmax(s):
        ms = [jnp.max(x, axis=-1, keepdims=True) for x in s]
        es = [jnp.exp(x - m) for x, m in zip(s, ms)]
        invs = [1.0 / jnp.sum(e, axis=-1, keepdims=True) for e in es]
        p1, p2, p3 = (e * i for e, i in zip(es, invs))
        return (p1 - lam * p2).astype(BF16), p3.astype(BF16)

    def stage_values(sl, p):
        o = jnp.dot(p[0], va_ref[:, sl].astype(BF16), preferred_element_type=F32)
        ob = jnp.dot(p[1], vb_ref[:, sl].astype(BF16), preferred_element_type=F32)
        y = o * lax.rsqrt(jnp.mean(o * o, axis=-1, keepdims=True) + EPS) * g_ref[...]
        oa_ref[:, sl] = (y * (1.0 - lam_init)).astype(oa_ref.dtype)
        ob_ref[:, sl] = ob.astype(ob_ref.dtype)

    scores, probs = {}, {}
    for t in range(heads + 2):
        if t < heads:
            scores[t] = stage_scores(sls[t])
        if 0 <= t - 1 < heads:
            probs[t - 1] = stage_softmax(scores.pop(t - 1))
        if 0 <= t - 2 < heads:
            stage_values(sls[t - 2], probs.pop(t - 2))
    av_ref[...] = va_ref[...].reshape(seq, heads, D_HEAD)
    bk_ref[...] = kb_ref[...].reshape(seq, heads, D_HEAD)
    bv_ref[...] = vb_ref[...].reshape(seq, heads, D_HEAD)


def ctx_ab_attention(p, seq, n_heads, diff_lambda, subln_g, lam_init, heads=8):
    m = p.shape[0]
    nb = m // seq
    nhb = n_heads // heads
    w = heads * D_HEAD
    blk = lambda part: pl.BlockSpec((seq, w), lambda b, h, part=part: (b, part * nhb + h))
    out2d = pl.BlockSpec((seq, w), lambda b, h: (b, h))
    cache = pl.BlockSpec((seq, heads, D_HEAD), lambda b, h: (b, h, 0))
    return pl.pallas_call(
        functools.partial(_ctx_ab_kernel, lam_init=lam_init, heads=heads),
        grid=(nb, nhb),
        in_specs=[blk(0), blk(1), blk(2), blk(3), blk(4), blk(5),
                  pl.BlockSpec((4, DQK_A), lambda b, h: (0, 0)),
                  pl.BlockSpec((1, D_HEAD), lambda b, h: (0, 0))],
        out_specs=[out2d, out2d, pl.BlockSpec((1, w, seq), lambda b, h: (b, h, 0)), cache, cache, cache],
        out_shape=[jax.ShapeDtypeStruct((m, n_heads * D_HEAD), BF16),
                   jax.ShapeDtypeStruct((m, n_heads * D_HEAD), BF16),
                   jax.ShapeDtypeStruct((nb, n_heads * D_HEAD, seq), F32),
                   jax.ShapeDtypeStruct((m, n_heads, D_HEAD), F32),
                   jax.ShapeDtypeStruct((m, n_heads, D_HEAD), F32),
                   jax.ShapeDtypeStruct((m, n_heads, D_HEAD), F32)],
        compiler_params=_params("parallel", "parallel"),
        name="ctx_ab_attention",
    )(p, p, p, p, p, p, diff_lambda, subln_g.reshape(1, D_HEAD))


def _na_bias_kernel(rpb_ref, o_ref, *, n_rows):
    h = pl.program_id(0)
    nd = 2 * WIN_C - 1
    ne = 2 * WIN_R - 1
    shape = (GRID_W, LANES)
    lane = lax.broadcasted_iota(jnp.int32, shape, 1)
    qc = lax.broadcasted_iota(jnp.int32, shape, 0)
    kc = lane & (GRID_W - 1)
    diff = kc - qc + (WIN_C - 1)
    cs = jnp.clip(qc - WIN_C // 2, 0, GRID_W - WIN_C)
    col_ok = (kc >= cs) & (kc < cs + WIN_C)
    low_half = lane < GRID_W
    neg = jnp.full(shape, -jnp.inf, F32)

    toeplitz = []
    for e in range(ne):
        acc = jnp.zeros(shape, F32)
        for d in range(nd):
            acc = jnp.where(diff == d, rpb_ref[h, e * nd + d], acc)
        toeplitz.append(jnp.where(col_ok, acc, neg))

    wr = min(WIN_R, n_rows)
    for r in range(n_rows):
        rs = min(max(r - wr // 2, 0), n_rows - wr)
        for pair in range(n_rows // 2):
            kr0, kr1 = 2 * pair, 2 * pair + 1
            a = toeplitz[kr0 - r + WIN_R - 1] if rs <= kr0 < rs + wr else neg
            b = toeplitz[kr1 - r + WIN_R - 1] if rs <= kr1 < rs + wr else neg
            blk = a if a is b else jnp.where(low_half, a, b)
            o_ref[0, r * GRID_W:(r + 1) * GRID_W, pair * LANES:(pair + 1) * LANES] = blk


def na_bias_table(rpb, n_rows):
    n_heads = rpb.shape[0]
    t = n_rows * GRID_W
    return pl.pallas_call(
        functools.partial(_na_bias_kernel, n_rows=n_rows),
        grid=(n_heads,),
        in_specs=[pl.BlockSpec(memory_space=pltpu.SMEM)],
        out_specs=pl.BlockSpec((1, t, t), lambda h: (h, 0, 0)),
        out_shape=jax.ShapeDtypeStruct((n_heads, t, t), F32),
        compiler_params=_params("parallel"),
        name="na_bias_table",
    )(rpb.reshape(n_heads, -1))


def _lat_ab_kernel(qa_ref, ka_ref, va_ref, qb_ref, kb_ref, vb_ref,
                   cak_ref, cav_ref, cbk_ref, cbv_ref, bias_ref,
                   qcos_ref, qslo_ref, qshi_ref, kcos_ref, kslo_ref, kshi_ref,
                   lp_ref, g_ref, oa_ref, ob_ref, *, lam_init):
    lam = _diff_lambda(lp_ref, lam_init)
    q = _rope_a(qa_ref[...], qcos_ref[...], qslo_ref[...], qshi_ref[...])
    k = _rope_a(ka_ref[...], kcos_ref[...], kslo_ref[...], kshi_ref[...])
    scores = _ab_scores(q, [k.astype(BF16), cak_ref[0].astype(BF16)],
                        qb_ref[...], [kb_ref[...].astype(BF16), cbk_ref[0].astype(BF16)], [bias_ref[0], None])
    parts = _exp_parts(scores)
    oa, ob = _ab_values(parts, [va_ref[...].astype(BF16), cav_ref[0].astype(BF16)],
                        [vb_ref[...].astype(BF16), cbv_ref[0].astype(BF16)], lam, g_ref[...], lam_init)
    oa_ref[...] = oa.astype(oa_ref.dtype)
    ob_ref[...] = ob.astype(ob_ref.dtype)


def lat_ab_attention(p, seq, n_heads, cache_a_k, cache_a_v, cache_b_k, cache_b_v, bias, rope,
                     diff_lambda, subln_g, lam_init, tq=1024):
    m = p.shape[0]
    nb = m // seq
    nq = seq // tq
    past = cache_a_k.shape[1]
    qblk = lambda part: pl.BlockSpec((tq, D_HEAD), lambda b, h, i, part=part: (b * nq + i, part * n_heads + h))
    kblk = lambda part: pl.BlockSpec((seq, D_HEAD), lambda b, h, i, part=part: (b, part * n_heads + h))
    cblk = pl.BlockSpec((1, past, D_HEAD), lambda b, h, i: (b, 0, h))
    qtab = pl.BlockSpec((tq, D_HEAD), lambda b, h, i: (i, 0))
    ktab = pl.BlockSpec((seq, D_HEAD), lambda b, h, i: (0, 0))
    cos, sin_lo, sin_hi = rope
    return pl.pallas_call(
        functools.partial(_lat_ab_kernel, lam_init=lam_init),
        grid=(nb, n_heads, nq),
        in_specs=[qblk(0), kblk(1), kblk(2), qblk(3), kblk(4), kblk(5),
                  cblk, cblk, cblk, cblk,
                  pl.BlockSpec((1, tq, seq), lambda b, h, i: (h, i, 0)),
                  qtab, qtab, qtab, ktab, ktab, ktab,
                  pl.BlockSpec((4, DQK_A), lambda b, h, i: (0, 0)),
                  pl.BlockSpec((1, D_HEAD), lambda b, h, i: (0, 0))],
        out_specs=[pl.BlockSpec((tq, D_HEAD), lambda b, h, i: (b * nq + i, h))] * 2,
        out_shape=[jax.ShapeDtypeStruct((m, n_heads * D_HEAD), BF16)] * 2,
        compiler_params=_params("parallel", "parallel", "parallel"),
        name="lat_ab_attention",
    )(p, p, p, p, p, p, cache_a_k, cache_a_v, cache_b_k, cache_b_v, bias,
      cos, sin_lo, sin_hi, cos, sin_lo, sin_hi, diff_lambda, subln_g.reshape(1, D_HEAD))


def _decay_geometry(row0, tq, tk):
    i = lax.broadcasted_iota(jnp.int32, (tq, tk), 0) + row0
    j = lax.broadcasted_iota(jnp.int32, (tq, tk), 1)
    rel = i - j
    return jnp.abs(rel).astype(F32), rel >= 0, jnp.where(rel == 0, 2.0, 1.0)


def _decay_mask(geometry, lg_f, lg_b):
    dist, is_fwd, diag = geometry
    return jnp.exp(jnp.where(is_fwd, lg_f, lg_b) * dist) * diag


def _gated_head_norm(o, g, norm_g):
    y = o * lax.rsqrt(jnp.mean(o * o, axis=-1, keepdims=True) + EPS) * norm_g
    return _silu(g.astype(F32)) * y


def _ctx_ret_kernel(lg_ref, q_ref, k_ref, v_ref, g_ref, ng_ref, o_ref, sf_ref, sb_ref, *, heads):
    t = q_ref.shape[0]
    pos = lax.broadcasted_iota(jnp.int32, (t, 1), 0).astype(F32)
    geometry = _decay_geometry(0, t, t)
    for hh in range(heads):
        h = pl.program_id(1) * heads + hh
        sl = slice(hh * DK_C, (hh + 1) * DK_C)
        lg_f = lg_ref[0, h]
        lg_b = lg_ref[1, h]
        k = k_ref[:, sl] * (DK_C ** -0.5)
        v = v_ref[:, sl]
        s = lax.dot_general(q_ref[:, sl], k, NT_DIMS, preferred_element_type=F32)
        att = (s * _decay_mask(geometry, lg_f, lg_b)).astype(BF16)
        o = jnp.dot(att, v, preferred_element_type=F32)
        o_ref[:, sl] = _gated_head_norm(o, g_ref[:, sl], ng_ref[...]).astype(o_ref.dtype)
        kf32 = k.astype(F32)
        kf = (kf32 * jnp.exp(lg_f * (t - 1.0 - pos))).T.astype(BF16)
        kb = (kf32 * jnp.exp(lg_b * pos)).T.astype(BF16)
        sf_ref[hh] = jnp.dot(kf, v, preferred_element_type=F32)
        sb_ref[hh] = jnp.dot(kb, v, preferred_element_type=F32)


def ctx_retention(p, seq, n_heads, lg, norm_g, heads=8):
    m = p.shape[0]
    nb = m // seq
    nhb = n_heads // heads
    w = heads * DK_C
    blk = lambda part: pl.BlockSpec((seq, w), lambda b, h, part=part: (b, part * nhb + h))
    state = pl.BlockSpec((None, None, heads, DK_C, DK_C), lambda b, h: (b, 0, h, 0, 0))
    return pl.pallas_call(
        functools.partial(_ctx_ret_kernel, heads=heads),
        grid=(nb, nhb),
        in_specs=[pl.BlockSpec(memory_space=pltpu.SMEM), blk(0), blk(1), blk(2), blk(3),
                  pl.BlockSpec((1, DK_C), lambda b, h: (0, 0))],
        out_specs=[pl.BlockSpec((seq, w), lambda b, h: (b, h)), state, state],
        out_shape=[jax.ShapeDtypeStruct((m, n_heads * DK_C), BF16),
                   jax.ShapeDtypeStruct((nb, 1, n_heads, DK_C, DK_C), F32),
                   jax.ShapeDtypeStruct((nb, 1, n_heads, DK_C, DK_C), F32)],
        compiler_params=_params("parallel", "parallel"),
        name="ctx_retention",
    )(lg, p, p, p, p, norm_g.reshape(1, DK_C))


def _rope_c(x, cos, sin):
    halves = []
    for c in range(DK_C // LANES):
        sl = slice(c * LANES, (c + 1) * LANES)
        xs = x[:, sl]
        halves.append(xs * cos[:, sl] + pltpu.roll(xs, LANES // 2, 1) * sin[:, sl])
    return jnp.concatenate(halves, axis=-1)


def _lat_ret_kernel(lg_ref, q_ref, k_ref, v_ref, g_ref, s0f_ref, s0b_ref, ng_ref,
                    qcos_ref, qsin_ref, kcos_ref, ksin_ref, o_ref, *, tq):
    h = pl.program_id(1)
    row0 = pl.program_id(2) * tq
    lg_f = lg_ref[0, h]
    lg_b = lg_ref[1, h]
    t = k_ref.shape[0]
    q = _rope_c(q_ref[...].astype(F32), qcos_ref[...], qsin_ref[...]).astype(BF16)
    k = _rope_c(k_ref[...].astype(F32) * (DK_C ** -0.5), kcos_ref[...], ksin_ref[...]).astype(BF16)
    s = lax.dot_general(q, k, NT_DIMS, preferred_element_type=F32)
    att = (s * _decay_mask(_decay_geometry(row0, tq, t), lg_f, lg_b)).astype(BF16)
    o = jnp.dot(att, v_ref[...], preferred_element_type=F32)
    pos = (lax.broadcasted_iota(jnp.int32, (tq, 1), 0) + row0).astype(F32)
    o += jnp.dot(q, s0f_ref[...].astype(BF16), preferred_element_type=F32) * jnp.exp(lg_f * (pos + 1.0))
    o += jnp.dot(q, s0b_ref[...].astype(BF16), preferred_element_type=F32) * jnp.exp(lg_b * (t - pos))
    o_ref[...] = _gated_head_norm(o, g_ref[...], ng_ref[...]).astype(o_ref.dtype)


def lat_retention(p, seq, n_heads, lg, norm_g, s0_f, s0_b, rope, tq=1024):
    m = p.shape[0]
    nb = m // seq
    nq = seq // tq
    qblk = lambda part: pl.BlockSpec((tq, DK_C), lambda b, h, i, part=part: (b * nq + i, part * n_heads + h))
    kblk = lambda part: pl.BlockSpec((seq, DK_C), lambda b, h, i, part=part: (b, part * n_heads + h))
    state = pl.BlockSpec((None, None, None, DK_C, DK_C), lambda b, h, i: (b, 0, h, 0, 0))
    qtab = pl.BlockSpec((tq, DK_C), lambda b, h, i: (i, 0))
    ktab = pl.BlockSpec((seq, DK_C), lambda b, h, i: (0, 0))
    cos, sin = rope
    return pl.pallas_call(
        functools.partial(_lat_ret_kernel, tq=tq),
        grid=(nb, n_heads, nq),
        in_specs=[pl.BlockSpec(memory_space=pltpu.SMEM), qblk(0), kblk(1), kblk(2), qblk(3),
                  state, state, pl.BlockSpec((1, DK_C), lambda b, h, i: (0, 0)),
                  qtab, qtab, ktab, ktab],
        out_specs=pl.BlockSpec((tq, DK_C), lambda b, h, i: (b * nq + i, h)),
        out_shape=jax.ShapeDtypeStruct((m, n_heads * DK_C), BF16),
        compiler_params=_params("parallel", "parallel", "parallel"),
        name="lat_retention",
    )(lg, p, p, p, p, s0_f, s0_b, norm_g.reshape(1, DK_C), cos, sin, cos, sin)


def _rope_tables(seq, width, group):
    half = group // 2
    t = np.arange(seq)
    rows, cols = t // GRID_W, t % GRID_W
    lane = np.arange(width)
    inv = (ROPE_BASE ** (-jnp.arange(half, dtype=F32) / half))[lane % half]
    use_cols = (lane // group) % 2 == 1
    pos = np.where(use_cols[None, :], cols[:, None], rows[:, None])
    ang = jnp.asarray(pos, dtype=jnp.int32).astype(F32) * inv[None, :]
    first = jnp.asarray((lane % group) < half)
    return jnp.cos(ang), jnp.sin(ang), first


def rope_tables_a(seq):
    cos, sin, first = _rope_tables(seq, D_HEAD, DQK_A // 2)
    return cos, jnp.where(first, -sin, 0.0), jnp.where(first, 0.0, sin)


def rope_tables_c(seq):
    cos, sin, first = _rope_tables(seq, DK_C, DK_C // 2)
    return cos, jnp.where(first, -sin, sin)


def kernel(x_prompt, x_sample, c, cache_a_k, cache_a_v, cache_b_k, cache_b_v, state_c_fwd, state_c_bwd,
           c_ctx, norm_mix_g, norm_mlp_g, w_mod, b_mod, w_mlp_in, w_mlp_out, w_in_ab, w_out_ab,
           diff_lambda, diff_subln_g, nat_rpb, w_in_c, w_out_c, ret_log_decay_fwd, ret_log_decay_bwd,
           ret_norm_g, final_norm_g):
    bp, seq_p, d = x_prompt.shape
    bs, seq_s, _ = x_sample.shape
    depth = w_mod.shape[0]
    n_heads_ab = w_out_ab.shape[1] // (2 * D_HEAD)
    n_heads_c = w_out_c.shape[1] // DK_C
    past = cache_a_k.shape[2]
    w_ab = n_heads_ab * D_HEAD

    xp = x_prompt.reshape(bp * seq_p, d)
    xs = x_sample.reshape(bs * seq_s, d)
    rows_p = bp * seq_p

    cvec = jnp.concatenate([c_ctx[None, :], c, jnp.zeros((MOD_ROWS - 1 - bs, d), F32)], axis=0)
    mod = mod_vectors(cvec, w_mod, b_mod).reshape(depth, MOD_ROWS, 6, 1, d)

    new_a_k, new_a_v, new_b_k, new_b_v, new_c_f, new_c_b = [], [], [], [], [], []

    for l in range(depth):
        sh1_p, sc1_p, g1_p, sh2_p, sc2_p, g2_p = (mod[l, 0:1, i] for i in range(6))
        sh1_s, sc1_s, g1_s, sh2_s, sc2_s, g2_s = (mod[l, 1:1 + bs, i] for i in range(6))
        hp = norm_modulate(xp, norm_mix_g[l], sh1_p, sc1_p, rows_p)
        hs = norm_modulate(xs, norm_mix_g[l], sh1_s, sc1_s, seq_s)

        if l % 2 == 0:
            e = l // 2
            lam_init = 0.8 - 0.6 * float(np.exp(-0.3 * l))
            w_out, w_out_layer = w_out_ab, e
            ps, w_in = matmul(hs, w_in_ab, e, emit_weights=True)
            pp = matmul(hp, w_in)
            oa, ob, akt, av, bk, bv = ctx_ab_attention(pp, seq_p, n_heads_ab, diff_lambda[e],
                                                       diff_subln_g[e], lam_init)
            mix_p = [oa, ob]
            new_a_k.append(akt.reshape(bp, 2 * n_heads_ab, DQK_A, seq_p).transpose(0, 3, 1, 2))
            new_a_v.append(av.reshape(bp, seq_p, n_heads_ab, D_HEAD))
            new_b_k.append(bk.reshape(bp, seq_p, n_heads_ab, D_HEAD))
            new_b_v.append(bv.reshape(bp, seq_p, n_heads_ab, D_HEAD))
            bias = na_bias_table(nat_rpb[e], seq_s // GRID_W)
            mix_s = lat_ab_attention(
                ps, seq_s, n_heads_ab,
                cache_a_k[:, e].reshape(bs, past, w_ab), cache_a_v[:, e].reshape(bs, past, w_ab),
                cache_b_k[:, e].reshape(bs, past, w_ab), cache_b_v[:, e].reshape(bs, past, w_ab),
                bias, rope_tables_a(seq_s), diff_lambda[e], diff_subln_g[e], lam_init)
        else:
            o = l // 2
            w_out, w_out_layer = w_out_c, o
            lg = jnp.stack([ret_log_decay_fwd[o], ret_log_decay_bwd[o]])
            ps, w_in = matmul(hs, w_in_c, o, out_dtype=BF16, emit_weights=True)
            pp = matmul(hp, w_in, out_dtype=BF16, tn=1024)
            mix_p, s_f, s_b = ctx_retention(pp, seq_p, n_heads_c, lg, ret_norm_g[o])
            new_c_f.append(s_f)
            new_c_b.append(s_b)
            mix_s = lat_retention(ps, seq_s, n_heads_c, lg, ret_norm_g[o],
                                  state_c_fwd[:, o:o + 1], state_c_bwd[:, o:o + 1], rope_tables_c(seq_s))

        xs, w_o = matmul(mix_s, w_out, w_out_layer, mode="resgate", res=xs, gate=g1_s, rows_per_group=seq_s,
                         emit_weights=True, tn=256)
        xp = matmul(mix_p, w_o, mode="resgate", res=xp, gate=g1_p, rows_per_group=rows_p)

        hp = norm_modulate(xp, norm_mlp_g[l], sh2_p, sc2_p, rows_p)
        hs = norm_modulate(xs, norm_mlp_g[l], sh2_s, sc2_s, seq_s)
        us, w1 = matmul(hs, w_mlp_in, l, mode="relu2", out_dtype=BF16, emit_weights=True)
        up = matmul(hp, w1, mode="relu2", out_dtype=BF16, tn=1024)
        xs, w2 = matmul(us, w_mlp_out, l, mode="resgate", res=xs, gate=g2_s, rows_per_group=seq_s,
                        emit_weights=True, tn=1024, tk=1024)
        xp = matmul(up, w2, mode="resgate", res=xp, gate=g2_p, rows_per_group=rows_p, tm=1024, tn=1024, tk=4096)

    y_prompt = rms_norm(xp, final_norm_g).reshape(bp, seq_p, d)
    y_sample = rms_norm(xs, final_norm_g).reshape(bs, seq_s, d)
    return (y_prompt, y_sample,
            jnp.stack(new_a_k, axis=1), jnp.stack(new_a_v, axis=1),
            jnp.stack(new_b_k, axis=1), jnp.stack(new_b_v, axis=1),
            jnp.concatenate(new_c_f, axis=1), jnp.concatenate(new_c_b, axis=1))
```
